```python
import math
import jax
import jax.numpy as jnp
from jax import lax
import numpy as np

D_MODEL = 1024
BATCH = 2
SEQ = 8192
DEPTH = 4
DEC_BATCH = 32
DEC_SEQ = 4
PAST_LEN = 8192
PAGE_SIZE = 128

N_MIXERS = 3
HEAD_DIM = 64
N_HEADS = D_MODEL // HEAD_DIM
DIL_PATTERNS = ((128, 1), (512, 4), (2048, 16))
N_GROUPS = len(DIL_PATTERNS)
HEADS_PER_GROUP = N_HEADS // 2
D_FF = 4 * D_MODEL
Q_BLOCK = 128
RMS_EPS = 1e-6
ALIBI_MAX_BIAS = 8.0
FORGET_BIAS_MEAN = 2.0
LAYER_MIXER = tuple(i % N_MIXERS for i in range(DEPTH))
LAYER_SLOT = tuple(sum(1 for j in range(i) if LAYER_MIXER[j] == LAYER_MIXER[i]) for i in range(DEPTH))
N_SB = LAYER_MIXER.count(0)
N_FOX = LAYER_MIXER.count(1)
N_DIL = LAYER_MIXER.count(2)

kernel_name = "hybrid_sb_fox_dilated_decode_step"


def _rmsnorm(x, g):
    xf = x.astype(jnp.float32)
    y = xf * lax.rsqrt(jnp.mean(xf * xf, axis=-1, keepdims=True) + RMS_EPS)
    return (y * g.astype(jnp.float32)).astype(x.dtype)


def _ffn(x, w_up, w_down):
    h = jnp.square(jax.nn.relu(x @ w_up))
    return h @ w_down


def _alibi_slopes():
    n = N_GROUPS * HEADS_PER_GROUP
    s = jnp.exp2(-ALIBI_MAX_BIAS * jnp.arange(1, n + 1, dtype=jnp.float32) / n)
    return s.reshape(N_GROUPS, HEADS_PER_GROUP)


def _gather_pages(cache, layer, page_table):
    rows = cache[layer, page_table]
    return rows.reshape((page_table.shape[0], page_table.shape[1] * PAGE_SIZE) + cache.shape[3:])


def _sweep_blocks(fn, seq_len):
    starts = jnp.arange(seq_len // Q_BLOCK, dtype=jnp.int32) * Q_BLOCK
    out = jnp.moveaxis(lax.map(fn, starts), 0, 1)
    return out.reshape((out.shape[0], seq_len) + out.shape[3:])


def _sb_attend(q, k, v, q_pos, k_pos):
    z = jnp.einsum('bqhd,bkhd->bhqk', q.astype(jnp.float32), k.astype(jnp.float32)) * (HEAD_DIM ** -0.5)
    mask = (k_pos[None, :] < q_pos[:, None])[None, None]
    log_beta = jax.nn.log_sigmoid(z)
    log_1mb = jnp.where(mask, jax.nn.log_sigmoid(-z), 0.0)
    between = lax.cumsum(log_1mb, axis=3, reverse=True) - log_1mb
    w = jnp.where(mask, jnp.exp(log_beta + between), 0.0)
    return jnp.einsum('bhqk,bkhd->bqhd', w, v.astype(jnp.float32)).astype(v.dtype)


def _fox_attend(q, k, v, cf_q, cf_k, q_pos, k_pos):
    s = jnp.einsum('bqhd,bkhd->bhqk', q.astype(jnp.float32), k.astype(jnp.float32)) * (HEAD_DIM ** -0.5)
    s = s + jnp.transpose(cf_q, (0, 2, 1))[..., :, None] - jnp.transpose(cf_k, (0, 2, 1))[..., None, :]
    s = jnp.where((k_pos[None, :] <= q_pos[:, None])[None, None], s, -jnp.inf)
    p = jax.nn.softmax(s, axis=-1)
    return jnp.einsum('bhqk,bkhd->bqhd', p, v.astype(jnp.float32)).astype(v.dtype)


def _dilated_group(q, kv_ext, rows, valid, dist, slopes):
    b, nq = q.shape[:2]
    nj = rows.shape[1]
    kv = jnp.take(kv_ext, rows.reshape(-1), axis=1).reshape((b, nq, nj) + kv_ext.shape[2:])
    s = jnp.einsum('bqhd,bqjhd->bhqj', q.astype(jnp.float32), kv[:, :, :, 0].astype(jnp.float32)) * (HEAD_DIM ** -0.5)
    s = s - slopes[:, None, None] * dist.astype(jnp.float32)[None, None, :]
    s = jnp.where(valid[None, None], s, -jnp.inf)
    m = jnp.max(s, axis=-1, keepdims=True)
    e = jnp.exp(s - m)
    l = jnp.sum(e, axis=-1, keepdims=True)
    o = jnp.einsum('bhqj,bqjhd->bhqd', e, kv[:, :, :, 1].astype(jnp.float32)) / l
    return m, l, o


def _combine_groups(parts):
    m = jnp.stack([p[0] for p in parts])
    l = jnp.stack([p[1] for p in parts])
    o = jnp.stack([p[2] for p in parts])
    den = jnp.exp(m - jnp.max(m, axis=0, keepdims=True)) * l
    w = den / jnp.sum(den, axis=0, keepdims=True)
    return jnp.sum(w * o, axis=0)


def _sb_mixer(xp, xs, cache_kv, layer, page_table, w_qkv, w_o):
    b, t = xp.shape[:2]
    qkv_p = (xp @ w_qkv).reshape(b, t, 3, N_HEADS, HEAD_DIM)
    q, k, v = qkv_p[:, :, 0], qkv_p[:, :, 1], qkv_p[:, :, 2]
    k_pos = jnp.arange(t)

    def block(t0):
        qb = lax.dynamic_slice_in_dim(q, t0, Q_BLOCK, axis=1)
        return _sb_attend(qb, k, v, t0 + jnp.arange(Q_BLOCK), k_pos)

    op = _sweep_blocks(block, t).reshape(b, t, N_HEADS * HEAD_DIM)
    bs, ts = xs.shape[:2]
    qkv_s = (xs @ w_qkv).reshape(bs, ts, 3, N_HEADS, HEAD_DIM)
    past = _gather_pages(cache_kv, layer, page_table)
    past_len = past.shape[1]
    kv_all = jnp.concatenate([past, qkv_s[:, :, 1:]], axis=1)
    os_ = _sb_attend(qkv_s[:, :, 0], kv_all[:, :, 0], kv_all[:, :, 1],
                     past_len + jnp.arange(ts), jnp.arange(past_len + ts))
    return op @ w_o, os_.reshape(bs, ts, N_HEADS * HEAD_DIM) @ w_o, qkv_p[:, :, 1:], qkv_s[:, :, 1:]


def _fox_mixer(xp, xs, cache_kv, cache_logf, layer, page_table, w_qkv, w_f, b_f, w_o):
    b, t = xp.shape[:2]
    qkv_p = (xp @ w_qkv).reshape(b, t, 3, N_HEADS, HEAD_DIM)
    q, k, v = qkv_p[:, :, 0], qkv_p[:, :, 1], qkv_p[:, :, 2]
    logf_p = jax.nn.log_sigmoid((xp @ w_f + b_f).astype(jnp.float32))
    cf_p = lax.cumsum(logf_p, axis=1)
    k_pos = jnp.arange(t)

    def block(t0):
        qb = lax.dynamic_slice_in_dim(q, t0, Q_BLOCK, axis=1)
        cfb = lax.dynamic_slice_in_dim(cf_p, t0, Q_BLOCK, axis=1)
        return _fox_attend(qb, k, v, cfb, cf_p, t0 + jnp.arange(Q_BLOCK), k_pos)

    op = _sweep_blocks(block, t).reshape(b, t, N_HEADS * HEAD_DIM)
    bs, ts = xs.shape[:2]
    qkv_s = (xs @ w_qkv).reshape(bs, ts, 3, N_HEADS, HEAD_DIM)
    logf_s = jax.nn.log_sigmoid((xs @ w_f + b_f).astype(jnp.float32))
    past = _gather_pages(cache_kv, layer, page_table)
    past_logf = _gather_pages(cache_logf, layer, page_table)
    past_len = past.shape[1]
    kv_all = jnp.concatenate([past, qkv_s[:, :, 1:]], axis=1)
    cf_all = lax.cumsum(jnp.concatenate([past_logf.astype(jnp.float32), logf_s], axis=1), axis=1)
    os_ = _fox_attend(qkv_s[:, :, 0], kv_all[:, :, 0], kv_all[:, :, 1], cf_all[:, past_len:], cf_all,
                      past_len + jnp.arange(ts), jnp.arange(past_len + ts))
    return (op @ w_o, os_.reshape(bs, ts, N_HEADS * HEAD_DIM) @ w_o,
            qkv_p[:, :, 1:], qkv_s[:, :, 1:], logf_p, logf_s)


def _dil_mixer(xp, xs, states, layer, w_qkv, w_o):
    slopes = _alibi_slopes()
    out_w = HEADS_PER_GROUP * HEAD_DIM
    b, t = xp.shape[:2]
    qkv_p = (xp @ w_qkv).reshape(b, t, 3, N_GROUPS, HEADS_PER_GROUP, HEAD_DIM)
    kv_pads = [jnp.pad(qkv_p[:, :, 1:, g], ((0, 0), (win, 0), (0, 0), (0, 0), (0, 0)))
               for g, (win, _) in enumerate(DIL_PATTERNS)]
    i_q = jnp.arange(Q_BLOCK)

    def block(t0):
        parts = []
        for g, (win, dil) in enumerate(DIL_PATTERNS):
            dist = jnp.arange(win // dil + 1) * dil
            pos = t0 + i_q[:, None] - dist[None, :]
            qb = lax.dynamic_slice_in_dim(qkv_p[:, :, 0, g], t0, Q_BLOCK, axis=1)
            parts.append(_dilated_group(qb, kv_pads[g], pos + win, pos >= 0, dist, slopes[g]))
        return jnp.moveaxis(_combine_groups(parts), 1, 2).astype(xp.dtype)

    op = _sweep_blocks(block, t).reshape(b, t, out_w)
    new_p = [kv_pads[g][:, -win:] for g, (win, _) in enumerate(DIL_PATTERNS)]
    bs, ts = xs.shape[:2]
    qkv_s = (xs @ w_qkv).reshape(bs, ts, 3, N_GROUPS, HEADS_PER_GROUP, HEAD_DIM)
    i_s = jnp.arange(ts)
    parts, new_s = [], []
    for g, (win, dil) in enumerate(DIL_PATTERNS):
        kv_ext = jnp.concatenate([states[g][layer], qkv_s[:, :, 1:, g]], axis=1)
        dist = jnp.arange(win // dil + 1) * dil
        rows = win + i_s[:, None] - dist[None, :]
        valid = (PAST_LEN + i_s[:, None] - dist[None, :]) >= 0
        parts.append(_dilated_group(qkv_s[:, :, 0, g], kv_ext, rows, valid, dist, slopes[g]))
        new_s.append(kv_ext[:, -win:])
    os_ = jnp.moveaxis(_combine_groups(parts), 1, 2).astype(xs.dtype).reshape(bs, ts, out_w)
    return op @ w_o, os_ @ w_o, new_p, new_s


def setup_inputs(seed: int = 0) -> dict:
    key = jax.random.key(seed)
    keys = iter(jax.random.split(key, 32))

    def nrm(shape, scale):
        return scale * jax.random.normal(next(keys), shape, jnp.float32)

    n_pages = PAST_LEN // PAGE_SIZE
    n_used = DEC_BATCH * n_pages
    n_pool = n_used + (n_used + 3) // 4
    hd = N_HEADS * HEAD_DIM
    dil_width = N_GROUPS * HEADS_PER_GROUP * HEAD_DIM
    dil_out = HEADS_PER_GROUP * HEAD_DIM
    x_prompt = nrm((BATCH, SEQ, D_MODEL), 1.0)
    x_sample = nrm((DEC_BATCH, DEC_SEQ, D_MODEL), 1.0)
    cache_sb_kv = nrm((N_SB, n_pool, PAGE_SIZE, 2, N_HEADS, HEAD_DIM), 1.0)
    cache_fox_kv = nrm((N_FOX, n_pool, PAGE_SIZE, 2, N_HEADS, HEAD_DIM), 1.0)
    cache_fox_logf = jax.nn.log_sigmoid(FORGET_BIAS_MEAN + nrm((N_FOX, n_pool, PAGE_SIZE, N_HEADS), 1.0))
    state_w128 = nrm((N_DIL, DEC_BATCH, DIL_PATTERNS[0][0], 2, HEADS_PER_GROUP, HEAD_DIM), 1.0)
    state_w512 = nrm((N_DIL, DEC_BATCH, DIL_PATTERNS[1][0], 2, HEADS_PER_GROUP, HEAD_DIM), 1.0)
    state_w2048 = nrm((N_DIL, DEC_BATCH, DIL_PATTERNS[2][0], 2, HEADS_PER_GROUP, HEAD_DIM), 1.0)
    page_table = jax.random.permutation(next(keys), n_pool)[:n_used].reshape(DEC_BATCH, n_pages).astype(jnp.int32)
    return {
        "x_prompt": x_prompt,
        "x_sample": x_sample,
        "cache_sb_kv": cache_sb_kv,
        "cache_fox_kv": cache_fox_kv,
        "cache_fox_logf": cache_fox_logf,
        "state_dil_kv_w128": state_w128,
        "state_dil_kv_w512": state_w512,
        "state_dil_kv_w2048": state_w2048,
        "page_table": page_table,
        "g_mix_pre": 1.0 + nrm((DEPTH, D_MODEL), 0.05),
        "g_mix_post": 1.0 + nrm((DEPTH, D_MODEL), 0.05),
        "g_ffn_pre": 1.0 + nrm((DEPTH, D_MODEL), 0.05),
        "g_ffn_post": 1.0 + nrm((DEPTH, D_MODEL), 0.05),
        "sb_w_qkv": nrm((N_SB, D_MODEL, 3 * hd), D_MODEL ** -0.5),
        "sb_w_o": nrm((N_SB, hd, D_MODEL), hd ** -0.5),
        "fox_w_qkv": nrm((N_FOX, D_MODEL, 3 * hd), D_MODEL ** -0.5),
        "fox_w_f": nrm((N_FOX, D_MODEL, N_HEADS), D_MODEL ** -0.5),
        "fox_b_f": FORGET_BIAS_MEAN + nrm((N_FOX, N_HEADS), 0.5),
        "fox_w_o": nrm((N_FOX, hd, D_MODEL), hd ** -0.5),
        "dil_w_qkv": nrm((N_DIL, D_MODEL, 3 * dil_width), D_MODEL ** -0.5),
        "dil_w_o": nrm((N_DIL, dil_out, D_MODEL), dil_out ** -0.5),
        "w_ffn_up": nrm((DEPTH, D_MODEL, D_FF), D_MODEL ** -0.5),
        "w_ffn_down": nrm((DEPTH, D_FF, D_MODEL), D_FF ** -0.5),
    }


def reference(x_prompt, x_sample, cache_sb_kv, cache_fox_kv, cache_fox_logf,
              state_dil_kv_w128, state_dil_kv_w512, state_dil_kv_w2048, page_table,
              g_mix_pre, g_mix_post, g_ffn_pre, g_ffn_post,
              sb_w_qkv, sb_w_o, fox_w_qkv, fox_w_f, fox_b_f, fox_w_o,
              dil_w_qkv, dil_w_o, w_ffn_up, w_ffn_down):
    dil_states = (state_dil_kv_w128, state_dil_kv_w512, state_dil_kv_w2048)
    yp, ys = x_prompt, x_sample
    sb_p, sb_s, fox_p, fox_s, fl_p, fl_s = [], [], [], [], [], []
    dil_p = [[] for _ in DIL_PATTERNS]
    dil_s = [[] for _ in DIL_PATTERNS]
    for i in range(DEPTH):
        kind, li = LAYER_MIXER[i], LAYER_SLOT[i]
        hp, hs = _rmsnorm(yp, g_mix_pre[i]), _rmsnorm(ys, g_mix_pre[i])
        if kind == 0:
            mp, ms, kvp, kvs = _sb_mixer(hp, hs, cache_sb_kv, li, page_table, sb_w_qkv[li], sb_w_o[li])
            sb_p.append(kvp)
            sb_s.append(kvs)
        elif kind == 1:
            mp, ms, kvp, kvs, lfp, lfs = _fox_mixer(hp, hs, cache_fox_kv, cache_fox_logf, li, page_table,
                                                    fox_w_qkv[li], fox_w_f[li], fox_b_f[li], fox_w_o[li])
            fox_p.append(kvp)
            fox_s.append(kvs)
            fl_p.append(lfp)
            fl_s.append(lfs)
        else:
            mp, ms, newp, news = _dil_mixer(hp, hs, dil_states, li, dil_w_qkv[li], dil_w_o[li])
            for g in range(N_GROUPS):
                dil_p[g].append(newp[g])
                dil_s[g].append(news[g])
        yp = yp + _rmsnorm(mp, g_mix_post[i])
        ys = ys + _rmsnorm(ms, g_mix_post[i])
        yp = yp + _rmsnorm(_ffn(_rmsnorm(yp, g_ffn_pre[i]), w_ffn_up[i], w_ffn_down[i]), g_ffn_post[i])
        ys = ys + _rmsnorm(_ffn(_rmsnorm(ys, g_ffn_pre[i]), w_ffn_up[i], w_ffn_down[i]), g_ffn_post[i])
    return (yp, ys,
            jnp.stack(sb_p), jnp.stack(sb_s),
            jnp.stack(fox_p), jnp.stack(fox_s), jnp.stack(fl_p), jnp.stack(fl_s),
            jnp.stack(dil_p[0]), jnp.stack(dil_s[0]),
            jnp.stack(dil_p[1]), jnp.stack(dil_s[1]),
            jnp.stack(dil_p[2]), jnp.stack(dil_s[2]))
```

```python
import functools

import jax
import jax.numpy as jnp
from jax import lax
from jax.experimental import pallas as pl
from jax.experimental.pallas import tpu as pltpu

F32 = jnp.float32
BF16 = jnp.bfloat16

HEAD_DIM = 64
PAGE = 128
N_MIXERS = 3
DIL_PATTERNS = ((128, 1), (512, 4), (2048, 16))
N_GROUPS = len(DIL_PATTERNS)
DIL_TAPS = 128
RMS_EPS = 1e-6
ALIBI_MAX_BIAS = 8.0
Q_SCALE = HEAD_DIM ** -0.5
MASKED = -1e30
EXP_ZERO = -104.0
ATT_BLOCK = 256
VMEM_LIMIT = 56 * 2 ** 20


def _params(sem):
    return pltpu.CompilerParams(dimension_semantics=sem, vmem_limit_bytes=VMEM_LIMIT)


def _nt(a, b):
    return lax.dot_general(a, b, (((1,), (1,)), ((), ())), preferred_element_type=F32)


def _dot(a, b):
    return jnp.dot(a, b, preferred_element_type=F32)


def _rms(x, g):
    return x * lax.rsqrt(jnp.mean(x * x, axis=-1, keepdims=True) + RMS_EPS) * g


def _softplus(z):
    return jnp.maximum(z, 0.0) + jnp.log(1.0 + jnp.exp(-jnp.abs(z)))


def _split3(x):
    a1 = x.astype(BF16)
    r1 = x - a1.astype(F32)
    a2 = r1.astype(BF16)
    a3 = (r1 - a2.astype(F32)).astype(BF16)
    return a1, a2, a3


def _tri(n, kind):
    j = lax.broadcasted_iota(jnp.int32, (n, n), 0)
    s = lax.broadcasted_iota(jnp.int32, (n, n), 1)
    m = (j > s) if kind == "gt" else (j <= s)
    return jnp.where(m, 1.0, 0.0).astype(BF16)


def _shr(x, n):
    assert n & (n - 1) == 0
    return jnp.right_shift(x, n.bit_length() - 1)


def _half_mask(shape, hh):
    lane = lax.broadcasted_iota(jnp.int32, shape, len(shape) - 1)
    return (lane < HEAD_DIM) if hh == 0 else (lane >= HEAD_DIM)


def _qkv_body(x_ref, g_ref, wq_ref, wkvT_ref, *rest, with_forget, chunk):
    if with_forget:
        wfT_ref, bf_ref, q_ref, kvT_ref, kvTb_ref, lfT_ref, cfT_ref, h_sc, carry_sc = rest
    else:
        q_ref, kvT_ref, kvTb_ref, h_sc = rest
    h_sc[...] = _rms(x_ref[...], g_ref[...]).astype(BF16)
    nq = wq_ref.shape[1]
    for c in range(0, nq, chunk):
        q_ref[:, c:c + chunk] = (_dot(h_sc[...], wq_ref[:, c:c + chunk]) * Q_SCALE).astype(BF16)
    nkv = wkvT_ref.shape[0]
    for r in range(0, nkv, chunk):
        t = _nt(wkvT_ref[r:r + chunk, :], h_sc[...])
        kvT_ref[r:r + chunk, :] = t
        kvTb_ref[r:r + chunk, :] = t.astype(BF16)
    if with_forget:
        tm = x_ref.shape[0]
        pre = _nt(wfT_ref[...], h_sc[...]) + bf_ref[...]
        lf = -_softplus(-pre)
        lfT_ref[...] = lf

        @pl.when(pl.program_id(1) == 0)
        def _():
            carry_sc[...] = jnp.zeros_like(carry_sc)

        tri = _tri(tm, "le")
        a1, a2, a3 = _split3(lf)
        cs = _dot(a1, tri) + _dot(a2, tri) + _dot(a3, tri) + carry_sc[...]
        cfT_ref[...] = cs
        carry_sc[...] = cs[:, tm - 1:tm]


def _qkv_proj(x, g, wq, wkvT, tm, forget=None):
    b, t, d = x.shape
    nq, nkv = wq.shape[1], wkvT.shape[0]
    nt = t // tm
    with_forget = forget is not None
    in_specs = [
        pl.BlockSpec((None, tm, d), lambda i, j: (i, j, 0)),
        pl.BlockSpec((1, d), lambda i, j: (0, 0)),
        pl.BlockSpec((d, nq), lambda i, j: (0, 0)),
        pl.BlockSpec((nkv, d), lambda i, j: (0, 0)),
    ]
    out_shape = [
        jax.ShapeDtypeStruct((b, t, nq), BF16),
        jax.ShapeDtypeStruct((b, nkv, t), F32),
        jax.ShapeDtypeStruct((b, nt, nkv, tm), BF16),
    ]
    out_specs = [
        pl.BlockSpec((None, tm, nq), lambda i, j: (i, j, 0)),
        pl.BlockSpec((None, nkv, tm), lambda i, j: (i, 0, j)),
        pl.BlockSpec((None, None, nkv, tm), lambda i, j: (i, j, 0, 0)),
    ]
    scratch = [pltpu.VMEM((tm, d), BF16)]
    args = [x, g.reshape(1, d), wq, wkvT]
    if with_forget:
        wfT, bf = forget
        nh = wfT.shape[0]
        in_specs += [pl.BlockSpec((nh, d), lambda i, j: (0, 0)),
                     pl.BlockSpec((nh, 1), lambda i, j: (0, 0))]
        out_shape += [jax.ShapeDtypeStruct((b, nh, t), F32)] * 2
        out_specs += [pl.BlockSpec((None, nh, tm), lambda i, j: (i, 0, j))] * 2
        scratch += [pltpu.VMEM((nh, 1), F32)]
        args += [wfT, bf.reshape(nh, 1)]
    return pl.pallas_call(
        functools.partial(_qkv_body, with_forget=with_forget, chunk=512),
        grid=(b, nt), in_specs=in_specs, out_specs=out_specs, out_shape=out_shape,
        scratch_shapes=scratch,
        compiler_params=_params(("parallel", "arbitrary")),
        name="qkv_proj_forget" if with_forget else "qkv_proj",
    )(*args)


def _oproj_body(a_ref, w_ref, g_ref, x_ref, y_ref):
    y_ref[...] = x_ref[...] + _rms(_dot(a_ref[...], w_ref[...]), g_ref[...])


def _out_proj(a, w, g, x, tm):
    n, k = a.shape
    d = w.shape[1]
    return pl.pallas_call(
        _oproj_body, grid=(n // tm,),
        in_specs=[pl.BlockSpec((tm, k), lambda i: (i, 0)),
                  pl.BlockSpec((k, d), lambda i: (0, 0)),
                  pl.BlockSpec((1, d), lambda i: (0, 0)),
                  pl.BlockSpec((tm, d), lambda i: (i, 0))],
        out_specs=pl.BlockSpec((tm, d), lambda i: (i, 0)),
        out_shape=jax.ShapeDtypeStruct((n, d), F32),
        compiler_params=_params(("parallel",)),
        name="out_proj",
    )(a, w, g.reshape(1, d), x)


def _ffn_body(x_ref, gpre_ref, wup_ref, wdn_ref, gpost_ref, y_ref, h_sc, acc_sc):
    j = pl.program_id(1)

    @pl.when(j == 0)
    def _():
        h_sc[...] = _rms(x_ref[...], gpre_ref[...]).astype(BF16)
        acc_sc[...] = jnp.zeros_like(acc_sc)

    u = jnp.maximum(_dot(h_sc[...], wup_ref[...]), 0.0)
    acc_sc[...] += _dot((u * u).astype(BF16), wdn_ref[...])

    @pl.when(j == pl.num_programs(1) - 1)
    def _():
        y_ref[...] = x_ref[...] + _rms(acc_sc[...], gpost_ref[...])


def _ffn(x, gpre, wup, wdn, gpost, tm, tf=512):
    n, d = x.shape
    f = wup.shape[1]
    return pl.pallas_call(
        _ffn_body, grid=(n // tm, f // tf),
        in_specs=[pl.BlockSpec((tm, d), lambda i, j: (i, 0)),
                  pl.BlockSpec((1, d), lambda i, j: (0, 0)),
                  pl.BlockSpec((d, tf), lambda i, j: (0, j)),
                  pl.BlockSpec((tf, d), lambda i, j: (j, 0)),
                  pl.BlockSpec((1, d), lambda i, j: (0, 0))],
        out_specs=pl.BlockSpec((tm, d), lambda i, j: (i, 0)),
        out_shape=jax.ShapeDtypeStruct((n, d), F32),
        scratch_shapes=[pltpu.VMEM((tm, d), BF16), pltpu.VMEM((tm, d), F32)],
        compiler_params=_params(("parallel", "arbitrary")),
        name="ffn",
    )(x, gpre.reshape(1, d), wup, wdn, gpost.reshape(1, d))


def _sb_prompt_body(q_ref, k_ref, v_ref, o_ref, *, tq):
    qi = pl.program_id(2)
    q = q_ref[...]
    row = lax.broadcasted_iota(jnp.int32, (tq, tq), 0)
    col = lax.broadcasted_iota(jnp.int32, (tq, tq), 1)
    strictly_before = col < row
    after = _tri(tq, "gt")
    outs = []
    for hh in range(2):
        qm = jnp.where(_half_mask(q.shape, hh), q, jnp.zeros_like(q))

        def block(kb, run, acc, diag):
            z = _dot(qm, k_ref[kb])
            sp = _softplus(z)
            l1 = jnp.where(strictly_before, -sp, 0.0) if diag else -sp
            between = _dot(l1.astype(BF16), after) + run
            w = jnp.exp((z - sp) + between)
            if diag:
                w = jnp.where(strictly_before, w, 0.0)
            acc = acc + _nt(w.astype(BF16), v_ref[kb])
            run = run + jnp.sum(l1, axis=1, keepdims=True)
            return run, acc

        run0, acc0 = block(qi, jnp.zeros((tq, 1), F32), jnp.zeros((tq, 128), F32), True)

        def cond(c):
            return jnp.logical_and(c[0] >= 0, c[3] > EXP_ZERO)

        def body(c):
            kb, run, acc, _ = c
            run, acc = block(kb, run, acc, False)
            return kb - 1, run, acc, jnp.max(run)

        _, _, acc, _ = lax.while_loop(cond, body, (qi - 1, run0, acc0, jnp.max(run0)))
        outs.append(acc)
    o_ref[...] = jnp.where(_half_mask(outs[0].shape, 0), outs[0], outs[1]).astype(o_ref.dtype)


def _sb_prompt_attention(q, kvTb):
    b, t, dq = q.shape
    nkb, tk = kvTb.shape[1], kvTb.shape[3]
    npair = dq // 128
    return pl.pallas_call(
        functools.partial(_sb_prompt_body, tq=tk),
        grid=(b, npair, t // tk),
        in_specs=[pl.BlockSpec((None, tk, 128), lambda i, p, j: (i, j, p)),
                  pl.BlockSpec((None, nkb, 128, tk), lambda i, p, j: (i, 0, p, 0)),
                  pl.BlockSpec((None, nkb, 128, tk), lambda i, p, j: (i, 0, npair + p, 0))],
        out_specs=pl.BlockSpec((None, tk, 128), lambda i, p, j: (i, j, p)),
        out_shape=jax.ShapeDtypeStruct((b, t, dq), BF16),
        compiler_params=_params(("parallel", "parallel", "arbitrary")),
        name="sb_prompt_attention",
    )(q, kvTb, kvTb)


def _softmax_block(s, vT, m_ref, l_ref, acc_ref):
    m_prev = m_ref[...]
    m_new = jnp.maximum(m_prev, jnp.max(s, axis=1, keepdims=True))
    alpha = jnp.exp(m_prev - m_new)
    p = jnp.exp(s - m_new)
    l_ref[...] = alpha * l_ref[...] + jnp.sum(p, axis=1, keepdims=True)
    acc_ref[...] = alpha * acc_ref[...] + _nt(p.astype(BF16), vT)
    m_ref[...] = m_new


def _softmax_init(m_ref, l_ref, acc_ref):
    m_ref[...] = jnp.full(m_ref.shape, MASKED, F32)
    l_ref[...] = jnp.zeros(l_ref.shape, F32)
    acc_ref[...] = jnp.zeros(acc_ref.shape, F32)


def _fox_prompt_body(q_ref, k_ref, v_ref, cf_ref, o_ref, m_sc, l_sc, acc_sc, *, tq):
    hp = pl.program_id(1)
    qi = pl.program_id(2)
    q = q_ref[...]
    row = lax.broadcasted_iota(jnp.int32, (tq, tq), 0)
    col = lax.broadcasted_iota(jnp.int32, (tq, tq), 1)
    causal = col <= row
    outs = []
    for hh in range(2):
        qm = jnp.where(_half_mask(q.shape, hh), q, jnp.zeros_like(q))
        h = 2 * hp + hh
        m_ref, l_ref, acc_ref = m_sc.at[hh], l_sc.at[hh], acc_sc.at[hh]
        _softmax_init(m_ref, l_ref, acc_ref)
        cfd = cf_ref[h, pl.ds(qi, 1), :]
        c0 = cfd[:, 0:1]
        s = _dot(qm, k_ref[qi]) + (c0 - cfd)
        _softmax_block(jnp.where(causal, s, MASKED), v_ref[qi], m_ref, l_ref, acc_ref)

        def body(kb, carry):
            s = _dot(qm, k_ref[kb]) + (c0 - cf_ref[h, pl.ds(kb, 1), :])
            _softmax_block(s, v_ref[kb], m_ref, l_ref, acc_ref)
            return carry

        lax.fori_loop(0, qi, body, 0)
        outs.append(acc_ref[...] / l_ref[...])
    o_ref[...] = jnp.where(_half_mask(outs[0].shape, 0), outs[0], outs[1]).astype(o_ref.dtype)


def _fox_prompt_attention(q, kvTb, cf):
    b, t, dq = q.shape
    nkb, tk = kvTb.shape[1], kvTb.shape[3]
    npair = dq // 128
    nh = cf.shape[1]
    return pl.pallas_call(
        functools.partial(_fox_prompt_body, tq=tk),
        grid=(b, npair, t // tk),
        in_specs=[pl.BlockSpec((None, tk, 128), lambda i, p, j: (i, j, p)),
                  pl.BlockSpec((None, nkb, 128, tk), lambda i, p, j: (i, 0, p, 0)),
                  pl.BlockSpec((None, nkb, 128, tk), lambda i, p, j: (i, 0, npair + p, 0)),
                  pl.BlockSpec((None, nh, nkb, tk), lambda i, p, j: (i, 0, 0, 0))],
        out_specs=pl.BlockSpec((None, tk, 128), lambda i, p, j: (i, j, p)),
        out_shape=jax.ShapeDtypeStruct((b, t, dq), BF16),
        scratch_shapes=[pltpu.VMEM((2, tk, 1), F32), pltpu.VMEM((2, tk, 1), F32),
                        pltpu.VMEM((2, tk, 128), F32)],
        compiler_params=_params(("parallel", "parallel", "arbitrary")),
        name="fox_prompt_attention",
    )(q, kvTb, kvTb, cf)


def _alibi_slope(head_index_f32, n_heads):
    return jnp.exp2(-ALIBI_MAX_BIAS * (head_index_f32 + 1.0) / n_heads)


def _dil_prompt_body(q0_ref, q1_ref, q2_ref, k0_ref, k1_ref, k2_ref, v0_ref, v1_ref, v2_ref,
                     o_ref, m_sc, l_sc, acc_sc, *, tq):
    hp = pl.program_id(1)
    qi = pl.program_id(2)
    q_refs, k_refs, v_refs = (q0_ref, q1_ref, q2_ref), (k0_ref, k1_ref, k2_ref), (v0_ref, v1_ref, v2_ref)
    row = lax.broadcasted_iota(jnp.int32, (tq, tq), 0)
    col = lax.broadcasted_iota(jnp.int32, (tq, tq), 1)
    rc = row - col
    heads_per_group = 4 * 2
    outs = []
    for hh in range(2):
        m_ref, l_ref, acc_ref = m_sc.at[hh], l_sc.at[hh], acc_sc.at[hh]
        _softmax_init(m_ref, l_ref, acc_ref)
        for g, (win, dil) in enumerate(DIL_PATTERNS):
            q = q_refs[g][...]
            qm = jnp.where(_half_mask(q.shape, hh), q, jnp.zeros_like(q))
            hidx = (g * heads_per_group + 2 * hp + hh).astype(F32)
            slope = _alibi_slope(jnp.full((1, tq), hidx, F32), N_GROUPS * heads_per_group)
            n_off = (win + tq - 1) // tq + 1
            for off in range(n_off):
                def blk(g=g, off=off, win=win, dil=dil, qm=qm, slope=slope):
                    kb = qi - off
                    dist = rc + off * tq
                    ok = jnp.logical_and(jnp.logical_and(dist >= 0, dist <= win), (dist & (dil - 1)) == 0)
                    s = _dot(qm, k_refs[g][kb]) - slope * dist.astype(F32)
                    _softmax_block(jnp.where(ok, s, MASKED), v_refs[g][kb], m_ref, l_ref, acc_ref)
                if off == 0:
                    blk()
                else:
                    pl.when(qi >= off)(blk)
        outs.append(acc_ref[...] / l_ref[...])
    o_ref[...] = jnp.where(_half_mask(outs[0].shape, 0), outs[0], outs[1]).astype(o_ref.dtype)


def _dil_prompt_attention(q, kvTb):
    b, t, dq = q.shape
    nkb, tk = kvTb.shape[1], kvTb.shape[3]
    gw = dq // N_GROUPS
    npair = gw // 128
    q_specs = [pl.BlockSpec((None, tk, 128), lambda i, p, j, g=g: (i, j, g * npair + p))
               for g in range(N_GROUPS)]
    k_specs = [pl.BlockSpec((None, nkb, 128, tk), lambda i, p, j, g=g: (i, 0, g * npair + p, 0))
               for g in range(N_GROUPS)]
    v_specs = [pl.BlockSpec((None, nkb, 128, tk),
                            lambda i, p, j, g=g: (i, 0, (N_GROUPS + g) * npair + p, 0))
               for g in range(N_GROUPS)]
    return pl.pallas_call(
        functools.partial(_dil_prompt_body, tq=tk),
        grid=(b, npair, t // tk),
        in_specs=q_specs + k_specs + v_specs,
        out_specs=pl.BlockSpec((None, tk, 128), lambda i, p, j: (i, j, p)),
        out_shape=jax.ShapeDtypeStruct((b, t, gw), BF16),
        scratch_shapes=[pltpu.VMEM((2, tk, 1), F32), pltpu.VMEM((2, tk, 1), F32),
                        pltpu.VMEM((2, tk, 128), F32)],
        compiler_params=_params(("parallel", "parallel", "arbitrary")),
        name="dil_prompt_attention",
    )(q, q, q, kvTb, kvTb, kvTb, kvTb, kvTb, kvTb)


def _head_of_col(shape):
    return _shr(lax.broadcasted_iota(jnp.int32, shape, 1), HEAD_DIM)


def _block_diag_queries(q, nh):
    nq, width = q.shape
    keep = lax.broadcasted_iota(jnp.int32, (nh, width), 0) == _head_of_col((nh, width))
    rows = [jnp.where(keep, jnp.broadcast_to(q[i:i + 1, :], (nh, width)), 0.0) for i in range(nq)]
    return jnp.concatenate(rows, axis=0).astype(BF16)


def _block_diag_extract(acc, nq, nh):
    width = acc.shape[1]
    keep = lax.broadcasted_iota(jnp.int32, (nh, width), 0) == _head_of_col((nh, width))
    rows = [jnp.sum(jnp.where(keep, acc[i * nh:(i + 1) * nh, :], 0.0), axis=0, keepdims=True)
            for i in range(nq)]
    return jnp.concatenate(rows, axis=0)


def _sb_decode_body(pt_ref, q_ref, new_ref, page_ref, o_ref, qbd_sc, run_sc, acc_sc, *, nq, nh):
    b = pl.program_id(0)
    p = pl.program_id(1)
    rows = nq * nh
    hd = nh * HEAD_DIM
    after = _tri(PAGE, "gt")

    def block(kT, vT, mask):
        z = _dot(qbd_sc[...], kT.astype(BF16))
        sp = _softplus(z)
        l1 = -sp if mask is None else jnp.where(mask, -sp, 0.0)
        between = _dot(l1.astype(BF16), after) + run_sc[...]
        w = jnp.exp((z - sp) + between)
        if mask is not None:
            w = jnp.where(mask, w, 0.0)
        acc_sc[...] += _nt(w.astype(BF16), vT.astype(BF16))
        run_sc[...] += jnp.sum(l1, axis=1, keepdims=True)

    @pl.when(p == 0)
    def _():
        qbd_sc[...] = _block_diag_queries(q_ref[...], nh)
        run_sc[...] = jnp.zeros_like(run_sc)
        acc_sc[...] = jnp.zeros_like(acc_sc)
        j = lax.broadcasted_iota(jnp.int32, (rows, PAGE), 1) - nq * b
        qrow = _shr(lax.broadcasted_iota(jnp.int32, (rows, PAGE), 0), nh)
        mask = jnp.logical_and(j >= 0, j < qrow)
        block(new_ref[0:hd, :], new_ref[hd:2 * hd, :], mask)

    @pl.when(p > 0)
    def _():
        block(page_ref[0], page_ref[1], None)

    @pl.when(p == pl.num_programs(1) - 1)
    def _():
        o_ref[...] = _block_diag_extract(acc_sc[...], nq, nh)


def _page_index(n_pages):
    def index(b, p, pt_ref):
        return pt_ref[b, n_pages - jnp.maximum(p, 1)]
    return index


def _sb_decode_attention(q, kvT_new, cacheT, layer, page_table):
    db, nq, hd = q.shape
    nh = hd // HEAD_DIM
    n_pages = page_table.shape[1]
    page_of = _page_index(n_pages)
    grid_spec = pltpu.PrefetchScalarGridSpec(
        num_scalar_prefetch=1, grid=(db, n_pages + 1),
        in_specs=[pl.BlockSpec((None, nq, hd), lambda b, p, pt: (b, 0, 0)),
                  pl.BlockSpec((2 * hd, db * nq), lambda b, p, pt: (0, 0)),
                  pl.BlockSpec((None, None, 2, hd, PAGE),
                               lambda b, p, pt: (layer, page_of(b, p, pt), 0, 0, 0))],
        out_specs=pl.BlockSpec((None, nq, hd), lambda b, p, pt: (b, 0, 0)),
        scratch_shapes=[pltpu.VMEM((nq * nh, hd), BF16), pltpu.VMEM((nq * nh, 1), F32),
                        pltpu.VMEM((nq * nh, hd), F32)])
    return pl.pallas_call(
        functools.partial(_sb_decode_body, nq=nq, nh=nh),
        grid_spec=grid_spec,
        out_shape=jax.ShapeDtypeStruct((db, nq, hd), F32),
        compiler_params=_params(("parallel", "arbitrary")),
        name="sb_decode_attention",
    )(page_table, q, kvT_new, cacheT)


def _fox_decode_body(pt_ref, q_ref, new_ref, lfnew_ref, page_ref, lfpage_ref, o_ref,
                     qbd_sc, tail_sc, m_sc, l_sc, acc_sc, *, nq, nh):
    b = pl.program_id(0)
    p = pl.program_id(1)
    rows = nq * nh
    hd = nh * HEAD_DIM
    after = _tri(PAGE, "gt")

    def block(kT, vT, lf, mask):
        a1, a2, a3 = _split3(lf)
        later = _dot(a1, after) + _dot(a2, after) + _dot(a3, after) + tail_sc[...]
        s = _dot(qbd_sc[...], kT.astype(BF16)) + jnp.concatenate([later] * nq, axis=0)
        if mask is not None:
            s = jnp.where(mask, s, MASKED)
        _softmax_block(s, vT.astype(BF16), m_sc, l_sc, acc_sc)
        tail_sc[...] += jnp.sum(lf, axis=1, keepdims=True)

    @pl.when(p == 0)
    def _():
        qbd_sc[...] = _block_diag_queries(q_ref[...], nh)
        tail_sc[...] = jnp.zeros_like(tail_sc)
        _softmax_init(m_sc, l_sc, acc_sc)
        j = lax.broadcasted_iota(jnp.int32, (rows, PAGE), 1) - nq * b
        qrow = _shr(lax.broadcasted_iota(jnp.int32, (rows, PAGE), 0), nh)
        mask = jnp.logical_and(j >= 0, j <= qrow)
        jl = lax.broadcasted_iota(jnp.int32, (nh, PAGE), 1) - nq * b
        own = jnp.logical_and(jl >= 0, jl < nq)
        block(new_ref[0:hd, :], new_ref[hd:2 * hd, :], jnp.where(own, lfnew_ref[...], 0.0), mask)

    @pl.when(p > 0)
    def _():
        block(page_ref[0], page_ref[1], lfpage_ref[...], None)

    @pl.when(p == pl.num_programs(1) - 1)
    def _():
        o_ref[...] = _block_diag_extract(acc_sc[...] / l_sc[...], nq, nh)


def _fox_decode_attention(q, kvT_new, lfT_new, cacheT, logfT, layer, page_table):
    db, nq, hd = q.shape
    nh = hd // HEAD_DIM
    n_pages = page_table.shape[1]
    page_of = _page_index(n_pages)
    grid_spec = pltpu.PrefetchScalarGridSpec(
        num_scalar_prefetch=1, grid=(db, n_pages + 1),
        in_specs=[pl.BlockSpec((None, nq, hd), lambda b, p, pt: (b, 0, 0)),
                  pl.BlockSpec((2 * hd, db * nq), lambda b, p, pt: (0, 0)),
                  pl.BlockSpec((nh, db * nq), lambda b, p, pt: (0, 0)),
                  pl.BlockSpec((None, None, 2, hd, PAGE),
                               lambda b, p, pt: (layer, page_of(b, p, pt), 0, 0, 0)),
                  pl.BlockSpec((None, None, nh, PAGE),
                               lambda b, p, pt: (layer, page_of(b, p, pt), 0, 0))],
        out_specs=pl.BlockSpec((None, nq, hd), lambda b, p, pt: (b, 0, 0)),
        scratch_shapes=[pltpu.VMEM((nq * nh, hd), BF16), pltpu.VMEM((nh, 1), F32),
                        pltpu.VMEM((nq * nh, 1), F32), pltpu.VMEM((nq * nh, 1), F32),
                        pltpu.VMEM((nq * nh, hd), F32)])
    return pl.pallas_call(
        functools.partial(_fox_decode_body, nq=nq, nh=nh),
        grid_spec=grid_spec,
        out_shape=jax.ShapeDtypeStruct((db, nq, hd), F32),
        compiler_params=_params(("parallel", "arbitrary")),
        name="fox_decode_attention",
    )(page_table, q, kvT_new, lfT_new, cacheT, logfT)


def _dil_decode_body(q_ref, new_ref, s0_ref, s1_ref, s2_ref, o_ref, n0_ref, n1_ref, n2_ref,
                     *, nq, nh, past_len):
    b = pl.program_id(0)
    rows = nq * nh
    gw = nh * HEAD_DIM
    state_refs, out_refs = (s0_ref, s1_ref, s2_ref), (n0_ref, n1_ref, n2_ref)
    n_new = new_ref.shape[1]
    qrow_n = _shr(lax.broadcasted_iota(jnp.int32, (rows, n_new), 0), nh)
    j_new = lax.broadcasted_iota(jnp.int32, (rows, n_new), 1) - nq * b
    parts = []
    for g, (win, dil) in enumerate(DIL_PATTERNS):
        qbd = _block_diag_queries(q_ref[:, g * gw:(g + 1) * gw], nh)
        hrow = (lax.broadcasted_iota(jnp.int32, (rows, 1), 0) & (nh - 1)) + g * nh
        slope = _alibi_slope(hrow.astype(F32), N_GROUPS * nh)
        st_ref = state_refs[g]
        kT_new = new_ref[g * gw:(g + 1) * gw, :]
        vT_new = new_ref[(N_GROUPS + g) * gw:(N_GROUPS + g + 1) * gw, :]
        qrow = _shr(lax.broadcasted_iota(jnp.int32, (rows, win), 0), nh)
        c = lax.broadcasted_iota(jnp.int32, (rows, win), 1)
        dist = win + qrow - c
        ok = jnp.logical_and((dist & (dil - 1)) == 0, dist <= jnp.minimum(win, past_len + qrow))
        s_old = _dot(qbd, st_ref[0:gw, :].astype(BF16)) - slope * dist.astype(F32)
        s_old = jnp.where(ok, s_old, MASKED)
        dist_n = qrow_n - j_new
        ok_n = jnp.logical_and(jnp.logical_and(j_new >= 0, dist_n >= 0), (dist_n & (dil - 1)) == 0)
        s_new = _dot(qbd, kT_new.astype(BF16)) - slope * dist_n.astype(F32)
        s_new = jnp.where(ok_n, s_new, MASKED)
        m = jnp.maximum(jnp.max(s_old, axis=1, keepdims=True), jnp.max(s_new, axis=1, keepdims=True))
        e_old = jnp.exp(s_old - m)
        e_new = jnp.exp(s_new - m)
        l = jnp.sum(e_old, axis=1, keepdims=True) + jnp.sum(e_new, axis=1, keepdims=True)
        o = _nt(e_old.astype(BF16), st_ref[gw:2 * gw, :].astype(BF16)) + \
            _nt(e_new.astype(BF16), vT_new.astype(BF16))
        parts.append((m, l, o))
        lane = lax.broadcasted_iota(jnp.int32, (gw, PAGE), 1)
        shift = (PAGE - nq) - nq * b
        shift = jnp.where(shift < 0, shift + PAGE, shift)
        for half, new_rows in ((0, kT_new), (1, vT_new)):
            old = st_ref[half * gw:(half + 1) * gw, :]
            rolled = pltpu.roll(old, win - nq, 1)
            tail = jnp.where(lane >= PAGE - nq, pltpu.roll(new_rows, shift, 1), rolled[:, win - PAGE:])
            out_refs[g][half * gw:(half + 1) * gw, :] = rolled
            out_refs[g][half * gw:(half + 1) * gw, win - PAGE:win] = tail
    m_all = functools.reduce(jnp.maximum, [pt[0] for pt in parts])
    den = sum(jnp.exp(pt[0] - m_all) * pt[1] for pt in parts)
    num = sum(jnp.exp(pt[0] - m_all) * pt[2] for pt in parts)
    o_ref[...] = _block_diag_extract(num / den, nq, nh)


def _dil_decode_attention(q, kvT_new, states, past_len):
    db, nq, dq = q.shape
    gw = dq // N_GROUPS
    nh = gw // HEAD_DIM
    st_specs = [pl.BlockSpec((None, 2 * gw, w), lambda b: (b, 0, 0)) for w, _ in DIL_PATTERNS]
    return pl.pallas_call(
        functools.partial(_dil_decode_body, nq=nq, nh=nh, past_len=past_len),
        grid=(db,),
        in_specs=[pl.BlockSpec((None, nq, dq), lambda b: (b, 0, 0)),
                  pl.BlockSpec(kvT_new.shape, lambda b: (0, 0))] + st_specs,
        out_specs=[pl.BlockSpec((None, nq, gw), lambda b: (b, 0, 0))] + st_specs,
        out_shape=[jax.ShapeDtypeStruct((db, nq, gw), F32)] +
                  [jax.ShapeDtypeStruct(s.shape, F32) for s in states],
        compiler_params=_params(("parallel",)),
        name="dil_decode_attention",
    )(q, kvT_new, *states)


def _tokens_minor(x):
    lead = x.shape[:-4]
    t, a, bb, c = x.shape[-4:]
    n = len(lead)
    xt = jnp.transpose(x, tuple(range(n)) + (n + 1, n + 2, n + 3, n))
    return xt.reshape(lead + (a * bb * c, t))


def _tokens_major(xT, a, bb, c):
    lead = xT.shape[:-2]
    t = xT.shape[-1]
    n = len(lead)
    x = xT.reshape(lead + (a, bb, c, t))
    return jnp.transpose(x, tuple(range(n)) + (n + 3, n, n + 1, n + 2))


def kernel(x_prompt, x_sample, cache_sb_kv, cache_fox_kv, cache_fox_logf, state_dil_kv_w128, state_dil_kv_w512, state_dil_kv_w2048, page_table, g_mix_pre, g_mix_post, g_ffn_pre, g_ffn_post, sb_w_qkv, sb_w_o, fox_w_qkv, fox_w_f, fox_b_f, fox_w_o, dil_w_qkv, dil_w_o, w_ffn_up, w_ffn_down):
    b, t, d = x_prompt.shape
    db, nq, _ = x_sample.shape
    depth = g_mix_pre.shape[0]
    nh = d // HEAD_DIM
    hd = nh * HEAD_DIM
    ns = db * nq
    past_len = page_table.shape[1] * PAGE
    dil_states = (state_dil_kv_w128, state_dil_kv_w512, state_dil_kv_w2048)
    gheads = dil_states[0].shape[-2]
    gw = gheads * HEAD_DIM
    assert t % ATT_BLOCK == 0 and ns == PAGE

    sb_cacheT = _tokens_minor(cache_sb_kv).reshape(cache_sb_kv.shape[:2] + (2, hd, PAGE))
    fox_cacheT = _tokens_minor(cache_fox_kv).reshape(cache_fox_kv.shape[:2] + (2, hd, PAGE))
    fox_logfT = jnp.swapaxes(cache_fox_logf, 2, 3)
    dil_statesT = [_tokens_minor(s) for s in dil_states]

    yp = x_prompt
    ys = x_sample.reshape(1, ns, d)
    outs = {k: [] for k in ("sb_p", "sb_s", "fox_p", "fox_s", "fl_p", "fl_s")}
    dil_p = [[] for _ in DIL_PATTERNS]
    dil_s = [[] for _ in DIL_PATTERNS]
    slot = [0] * N_MIXERS
    tm_p = 512
    for i in range(depth):
        kind = i % N_MIXERS
        li = slot[kind]
        slot[kind] += 1
        if kind == 2:
            w_qkv, w_o = dil_w_qkv[li], dil_w_o[li]
        elif kind == 1:
            w_qkv, w_o = fox_w_qkv[li], fox_w_o[li]
        else:
            w_qkv, w_o = sb_w_qkv[li], sb_w_o[li]
        nqf = w_qkv.shape[1] // 3
        wq = w_qkv[:, :nqf].astype(BF16)
        wkvT = w_qkv[:, nqf:].T.astype(BF16)
        w_o = w_o.astype(BF16)
        forget = (fox_w_f[li].T.astype(BF16), fox_b_f[li]) if kind == 1 else None
        res_p = _qkv_proj(yp, g_mix_pre[i], wq, wkvT, ATT_BLOCK, forget)
        res_s = _qkv_proj(ys, g_mix_pre[i], wq, wkvT, ns, forget)
        q_p, kvT_p, kvTb_p = res_p[:3]
        q_s = res_s[0].astype(F32).reshape(db, nq, nqf)
        kvT_s = res_s[1][0]
        if kind == 0:
            att_p = _sb_prompt_attention(q_p, kvTb_p)
            att_s = _sb_decode_attention(q_s, kvT_s, sb_cacheT, li, page_table)
            outs["sb_p"].append(_tokens_major(kvT_p, 2, nh, HEAD_DIM))
            outs["sb_s"].append(kvT_s.T.reshape(db, nq, 2, nh, HEAD_DIM))
        elif kind == 1:
            lfT_p, cfT_p = res_p[3], res_p[4]
            lfT_s = res_s[3][0]
            cf = cfT_p.reshape(b, nh, t // ATT_BLOCK, ATT_BLOCK)
            att_p = _fox_prompt_attention(q_p, kvTb_p, cf)
            att_s = _fox_decode_attention(q_s, kvT_s, lfT_s, fox_cacheT, fox_logfT, li, page_table)
            outs["fox_p"].append(_tokens_major(kvT_p, 2, nh, HEAD_DIM))
            outs["fox_s"].append(kvT_s.T.reshape(db, nq, 2, nh, HEAD_DIM))
            outs["fl_p"].append(jnp.swapaxes(lfT_p, 1, 2))
            outs["fl_s"].append(lfT_s.T.reshape(db, nq, nh))
        else:
            assert t >= max(w for w, _ in DIL_PATTERNS)
            att_p = _dil_prompt_attention(q_p, kvTb_p)
            att_s, *new_states = _dil_decode_attention(
                q_s, kvT_s, [s[li] for s in dil_statesT], past_len)
            kv5 = kvT_p.reshape(b, 2, N_GROUPS, gw, t)
            for g, (win, _) in enumerate(DIL_PATTERNS):
                dil_p[g].append(_tokens_major(kv5[:, :, g, :, t - win:].reshape(b, 2 * gw, win),
                                              2, gheads, HEAD_DIM))
                dil_s[g].append(_tokens_major(new_states[g], 2, gheads, HEAD_DIM))
        att_p = att_p.reshape(b * t, -1)
        att_s = att_s.reshape(ns, -1).astype(BF16)
        yp2 = _out_proj(att_p, w_o, g_mix_post[i], yp.reshape(b * t, d), tm_p)
        ys2 = _out_proj(att_s, w_o, g_mix_post[i], ys.reshape(ns, d), ns)
        wup, wdn = w_ffn_up[i].astype(BF16), w_ffn_down[i].astype(BF16)
        yp = _ffn(yp2, g_ffn_pre[i], wup, wdn, g_ffn_post[i], tm_p).reshape(b, t, d)
        ys = _ffn(ys2, g_ffn_pre[i], wup, wdn, g_ffn_post[i], ns).reshape(1, ns, d)
    return (yp, ys.reshape(db, nq, d),
            jnp.stack(outs["sb_p"]), jnp.stack(outs["sb_s"]),
            jnp.stack(outs["fox_p"]), jnp.stack(outs["fox_s"]),
            jnp.stack(outs["fl_p"]), jnp.stack(outs["fl_s"]),
            jnp.stack(dil_p[0]), jnp.stack(dil_s[0]),
            jnp.stack(dil_p[1]), jnp.stack(dil_s[1]),
            jnp.stack(dil_p[2]), jnp.stack(dil_s[2]))
```

```python
import functools

import jax
import jax.numpy as jnp
from jax import lax
from jax.experimental import pallas as pl
from jax.experimental.pallas import tpu as pltpu

F32 = jnp.float32
BF16 = jnp.bfloat16

HEAD_DIM = 64
LANES = 128
PAGE = 128
N_MIXERS = 3
DIL_PATTERNS = ((128, 1), (512, 4), (2048, 16))
N_GROUPS = len(DIL_PATTERNS)
RMS_EPS = 1e-6
ALIBI_MAX_BIAS = 8.0
Q_SCALE = HEAD_DIM ** -0.5
LOG2E = 1.4426950408889634
MASKED = -1e30
EXP_ZERO = -104.0
ATT_BLOCK = 256
FOX_CHUNK = 4
DIL_CHUNK = 3
PAGES_PER_STEP = 4
PROMPT_ROWS = 512
FFN_ROWS = 1024
FFN_COLS = 1024
VMEM_LIMIT = 56 * 2 ** 20


def _params(sem):
    return pltpu.CompilerParams(dimension_semantics=sem, vmem_limit_bytes=VMEM_LIMIT)


def _nt(a, b):
    return lax.dot_general(a, b, (((1,), (1,)), ((), ())), preferred_element_type=F32)


def _dot(a, b):
    return jnp.dot(a, b, preferred_element_type=F32)


def _rms(x, g):
    return x * lax.rsqrt(jnp.mean(x * x, axis=-1, keepdims=True) + RMS_EPS) * g


def _softplus(z):
    return jnp.maximum(z, 0.0) + jnp.log(1.0 + jnp.exp(-jnp.abs(z)))


def _split3(x):
    a1 = x.astype(BF16)
    r1 = x - a1.astype(F32)
    a2 = r1.astype(BF16)
    a3 = (r1 - a2.astype(F32)).astype(BF16)
    return a1, a2, a3


def _dot3_right(x, m):
    a1, a2, a3 = _split3(x)
    return _dot(a1, m) + _dot(a2, m) + _dot(a3, m)


def _dot3_left(m, x):
    a1, a2, a3 = _split3(x)
    return _dot(m, a1) + _dot(m, a2) + _dot(m, a3)


def _tri(n, kind):
    r = lax.broadcasted_iota(jnp.int32, (n, n), 0)
    c = lax.broadcasted_iota(jnp.int32, (n, n), 1)
    m = (r > c) if kind == "gt" else (r >= c)
    return jnp.where(m, 1.0, 0.0).astype(BF16)


def _shr(x, n):
    assert n & (n - 1) == 0
    return jnp.right_shift(x, n.bit_length() - 1)


def _half_mask(shape, hh, axis):
    idx = lax.broadcasted_iota(jnp.int32, shape, axis)
    return (idx < HEAD_DIM) if hh == 0 else (idx >= HEAD_DIM)


def _qkv_body(*refs, layout, with_forget, chunk, sub):
    it = iter(refs)
    x_ref, g_ref, wq_ref = next(it), next(it), next(it)
    wk_ref = next(it) if layout == "col" else None
    wkvT_ref = next(it)
    if with_forget:
        wfT_ref, wf_ref, bcol_ref, brow_ref = next(it), next(it), next(it), next(it)
    q_ref = next(it)
    k_ref = next(it) if layout == "col" else None
    kvT_ref, tb_ref = next(it), next(it)
    if with_forget:
        lfT_ref, cfb_ref = next(it), next(it)
    h_sc = next(it)
    if with_forget:
        carry_sc = next(it)

    tm = x_ref.shape[0]
    h_sc[...] = _rms(x_ref[...], g_ref[...]).astype(BF16)
    nkv = wkvT_ref.shape[0]
    if layout == "row":
        for c in range(0, wq_ref.shape[1], chunk):
            q_ref[:, c:c + chunk] = (_dot(h_sc[...], wq_ref[:, c:c + chunk]) * Q_SCALE).astype(BF16)
        tb_rows = (0, nkv)
    else:
        for r in range(0, wq_ref.shape[0], chunk):
            q_ref[r:r + chunk, :] = (_nt(wq_ref[r:r + chunk, :], h_sc[...]) * (Q_SCALE * LOG2E)).astype(BF16)
        for c in range(0, wk_ref.shape[1], chunk):
            k_ref[:, c:c + chunk] = _dot(h_sc[...], wk_ref[:, c:c + chunk]).astype(BF16)
        tb_rows = (nkv // 2, nkv)
    for r in range(0, nkv, chunk):
        t = _nt(wkvT_ref[r:r + chunk, :], h_sc[...])
        kvT_ref[r:r + chunk, :] = t
        if r >= tb_rows[0]:
            tb = t.astype(BF16)
            for s in range(tm // sub):
                tb_ref[s, r - tb_rows[0]:r - tb_rows[0] + chunk, :] = tb[:, s * sub:(s + 1) * sub]
    if with_forget:
        nh = wfT_ref.shape[0]
        lfT_ref[...] = -_softplus(-(_nt(wfT_ref[...], h_sc[...]) + bcol_ref[...]))
        lf = -_softplus(-(_dot(h_sc[...], wf_ref[...]) + brow_ref[...]))

        @pl.when(pl.program_id(1) == 0)
        def _():
            carry_sc[...] = jnp.zeros_like(carry_sc)

        cf = _dot3_left(_tri(tm, "ge"), lf) + carry_sc[...]
        carry_sc[...] = cf[tm - 1:tm, :]
        rep = lax.broadcasted_iota(jnp.int32, (nh, nh * LANES), 0) == \
            _shr(lax.broadcasted_iota(jnp.int32, (nh, nh * LANES), 1), LANES)
        cfb_ref[...] = _dot3_right(cf * LOG2E, jnp.where(rep, 1.0, 0.0).astype(BF16))


def _qkv_proj(x, g, w_qkv, tm, layout, forget=None):
    b, t, d = x.shape
    n = w_qkv.shape[1] // 3
    sub = min(tm, ATT_BLOCK)
    nt = t // tm
    with_forget = forget is not None
    const = lambda i, j: (0, 0)
    args = [x, g.reshape(1, d)]
    in_specs = [pl.BlockSpec((None, tm, d), lambda i, j: (i, j, 0)), pl.BlockSpec((1, d), const)]
    if layout == "row":
        args += [w_qkv[:, :n].astype(BF16)]
        in_specs += [pl.BlockSpec((d, n), const)]
        out_shape = [jax.ShapeDtypeStruct((b, t, n), BF16)]
        out_specs = [pl.BlockSpec((None, tm, n), lambda i, j: (i, j, 0))]
        tb_rows = 2 * n
    else:
        args += [w_qkv[:, :n].T.astype(BF16), w_qkv[:, n:2 * n].astype(BF16)]
        in_specs += [pl.BlockSpec((n, d), const), pl.BlockSpec((d, n), const)]
        out_shape = [jax.ShapeDtypeStruct((b, n, t), BF16), jax.ShapeDtypeStruct((b, t, n), BF16)]
        out_specs = [pl.BlockSpec((None, n, tm), lambda i, j: (i, 0, j)),
                     pl.BlockSpec((None, tm, n), lambda i, j: (i, j, 0))]
        tb_rows = n
    args += [w_qkv[:, n:].T.astype(BF16)]
    in_specs += [pl.BlockSpec((2 * n, d), const)]
    out_shape += [jax.ShapeDtypeStruct((b, 2 * n, t), F32),
                  jax.ShapeDtypeStruct((b, t // sub, tb_rows, sub), BF16)]
    out_specs += [pl.BlockSpec((None, 2 * n, tm), lambda i, j: (i, 0, j)),
                  pl.BlockSpec((None, tm // sub, tb_rows, sub), lambda i, j: (i, j, 0, 0))]
    scratch = [pltpu.VMEM((tm, d), BF16)]
    if with_forget:
        w_f, b_f = forget
        nh = w_f.shape[1]
        args += [w_f.T.astype(BF16), w_f.astype(BF16), b_f.reshape(nh, 1), b_f.reshape(1, nh)]
        in_specs += [pl.BlockSpec((nh, d), const), pl.BlockSpec((d, nh), const),
                     pl.BlockSpec((nh, 1), const), pl.BlockSpec((1, nh), const)]
        out_shape += [jax.ShapeDtypeStruct((b, nh, t), F32),
                      jax.ShapeDtypeStruct((b, t, nh * LANES), F32)]
        out_specs += [pl.BlockSpec((None, nh, tm), lambda i, j: (i, 0, j)),
                      pl.BlockSpec((None, tm, nh * LANES), lambda i, j: (i, j, 0))]
        scratch += [pltpu.VMEM((1, nh), F32)]
    return pl.pallas_call(
        functools.partial(_qkv_body, layout=layout, with_forget=with_forget, chunk=512, sub=sub),
        grid=(b, nt), in_specs=in_specs, out_specs=out_specs, out_shape=out_shape,
        scratch_shapes=scratch,
        compiler_params=_params(("parallel", "arbitrary")),
        name="qkv_proj_" + layout + ("_forget" if with_forget else ""),
    )(*args)


def _oproj_body(a_ref, w_ref, g_ref, x_ref, y_ref, *, transposed):
    if transposed:
        m = lax.dot_general(a_ref[...], w_ref[...], (((0,), (0,)), ((), ())),
                            preferred_element_type=F32)
    else:
        m = _dot(a_ref[...], w_ref[...])
    y_ref[...] = x_ref[...] + _rms(m, g_ref[...])


def _out_proj(a, w, g, x, tm, transposed=False):
    b, t, d = x.shape
    k = w.shape[0]
    if transposed:
        a_spec = pl.BlockSpec((None, k, tm), lambda i, j: (i, 0, j))
    else:
        a_spec = pl.BlockSpec((None, tm, k), lambda i, j: (i, j, 0))
    return pl.pallas_call(
        functools.partial(_oproj_body, transposed=transposed), grid=(b, t // tm),
        in_specs=[a_spec,
                  pl.BlockSpec((k, d), lambda i, j: (0, 0)),
                  pl.BlockSpec((1, d), lambda i, j: (0, 0)),
                  pl.BlockSpec((None, tm, d), lambda i, j: (i, j, 0))],
        out_specs=pl.BlockSpec((None, tm, d), lambda i, j: (i, j, 0)),
        out_shape=jax.ShapeDtypeStruct((b, t, d), F32),
        compiler_params=_params(("parallel", "parallel")),
        name="out_proj_t" if transposed else "out_proj",
    )(a, w, g.reshape(1, d), x)


def _ffn_body(x_ref, gpre_ref, wup_ref, wdn_ref, gpost_ref, y_ref, h_sc, acc_sc):
    j = pl.program_id(1)

    @pl.when(j == 0)
    def _():
        h_sc[...] = _rms(x_ref[...], gpre_ref[...]).astype(BF16)
        acc_sc[...] = jnp.zeros_like(acc_sc)

    u = jnp.maximum(_dot(h_sc[...], wup_ref[...]), 0.0)
    acc_sc[...] += _dot((u * u).astype(BF16), wdn_ref[...])

    @pl.when(j == pl.num_programs(1) - 1)
    def _():
        y_ref[...] = x_ref[...] + _rms(acc_sc[...], gpost_ref[...])


def _ffn(x, gpre, wup, wdn, gpost, tm, tf):
    n, d = x.shape
    f = wup.shape[1]
    return pl.pallas_call(
        _ffn_body, grid=(n // tm, f // tf),
        in_specs=[pl.BlockSpec((tm, d), lambda i, j: (i, 0)),
                  pl.BlockSpec((1, d), lambda i, j: (0, 0)),
                  pl.BlockSpec((d, tf), lambda i, j: (0, j)),
                  pl.BlockSpec((tf, d), lambda i, j: (j, 0)),
                  pl.BlockSpec((1, d), lambda i, j: (0, 0))],
        out_specs=pl.BlockSpec((tm, d), lambda i, j: (i, 0)),
        out_shape=jax.ShapeDtypeStruct((n, d), F32),
        scratch_shapes=[pltpu.VMEM((tm, d), BF16), pltpu.VMEM((tm, d), F32)],
        compiler_params=_params(("parallel", "arbitrary")),
        name="ffn",
    )(x, gpre.reshape(1, d), wup, wdn, gpost.reshape(1, d))


def _sb_prompt_body(q_ref, k_ref, v_ref, o_ref, *, tq):
    qi = pl.program_id(2)
    q = q_ref[...]
    row = lax.broadcasted_iota(jnp.int32, (tq, tq), 0)
    col = lax.broadcasted_iota(jnp.int32, (tq, tq), 1)
    strictly_before = col < row
    after = _tri(tq, "gt")
    qms = [jnp.where(_half_mask(q.shape, hh, 1), q, jnp.zeros_like(q)) for hh in range(2)]

    def block(kb, state, diag):
        kT, vT = k_ref[kb], v_ref[kb]
        new_state = []
        for hh in range(2):
            run, acc = state[2 * hh], state[2 * hh + 1]
            z = _dot(qms[hh], kT)
            sp = _softplus(z)
            l1 = jnp.where(strictly_before, -sp, 0.0) if diag else -sp
            between = _dot(l1.astype(BF16), after) + run
            w = jnp.exp((z - sp) + between)
            if diag:
                w = jnp.where(strictly_before, w, 0.0)
            new_state += [run + jnp.sum(l1, axis=1, keepdims=True),
                          acc + _nt(w.astype(BF16), vT)]
        return tuple(new_state)

    def highest_run(state):
        return jnp.maximum(jnp.max(state[0]), jnp.max(state[2]))

    zeros = (jnp.zeros((tq, 1), F32), jnp.zeros((tq, LANES), F32))
    state0 = block(qi, zeros + zeros, True)

    def cond(c):
        return jnp.logical_and(c[0] >= 0, c[1] > EXP_ZERO)

    def body(c):
        state = block(c[0], c[2:], False)
        return (c[0] - 1, highest_run(state)) + state

    final = lax.while_loop(cond, body, (qi - 1, highest_run(state0)) + state0)
    o_ref[...] = jnp.where(_half_mask((tq, LANES), 0, 1), final[3], final[5]).astype(o_ref.dtype)


def _sb_prompt_attention(q, kvTb):
    b, t, dq = q.shape
    nkb, tk = kvTb.shape[1], kvTb.shape[3]
    npair = dq // LANES
    return pl.pallas_call(
        functools.partial(_sb_prompt_body, tq=tk),
        grid=(b, npair, t // tk),
        in_specs=[pl.BlockSpec((None, tk, LANES), lambda i, p, j: (i, j, p)),
                  pl.BlockSpec((None, nkb, LANES, tk), lambda i, p, j: (i, 0, p, 0)),
                  pl.BlockSpec((None, nkb, LANES, tk), lambda i, p, j: (i, 0, npair + p, 0))],
        out_specs=pl.BlockSpec((None, tk, LANES), lambda i, p, j: (i, j, p)),
        out_shape=jax.ShapeDtypeStruct((b, t, dq), BF16),
        compiler_params=_params(("parallel", "parallel", "arbitrary")),
        name="sb_prompt_attention",
    )(q, kvTb, kvTb)


def _softmax_update_t(sTs, vTs, m_ref, l_ref, acc_ref):
    m_prev = m_ref[...]
    m_new = m_prev
    for sT in sTs:
        m_new = jnp.maximum(m_new, jnp.max(sT, axis=0, keepdims=True))
    alpha = jnp.exp2(m_prev - m_new)
    l = alpha * l_ref[...]
    acc = alpha * acc_ref[...]
    for sT, vT_list in zip(sTs, vTs):
        p = jnp.exp2(sT - m_new)
        l = l + jnp.sum(p, axis=0, keepdims=True)
        pb = p.astype(BF16)
        tk = vT_list[0].shape[1]
        for j, vT in enumerate(vT_list):
            acc = acc + _dot(vT, pb[j * tk:(j + 1) * tk, :])
    l_ref[...] = l
    acc_ref[...] = acc
    m_ref[...] = m_new


def _softmax_init(m_ref, l_ref, acc_ref):
    m_ref[...] = jnp.full(m_ref.shape, MASKED, F32)
    l_ref[...] = jnp.zeros(l_ref.shape, F32)
    acc_ref[...] = jnp.zeros(acc_ref.shape, F32)


def _pair_queries(qT):
    return [jnp.where(_half_mask(qT.shape, hh, 0), qT, jnp.zeros_like(qT)) for hh in range(2)]


def _write_pair(o_ref, l_sc, acc_sc):
    for hh in range(2):
        o_ref[hh * HEAD_DIM:(hh + 1) * HEAD_DIM, :] = (acc_sc[hh] / l_sc[hh]).astype(o_ref.dtype)


def _fox_prompt_body(qT_ref, k_ref, v_ref, cfb_ref, o_ref, m_sc, l_sc, acc_sc, *, tq, ch):
    qi = pl.program_id(2)
    qms = _pair_queries(qT_ref[...])
    tkc = ch * tq
    kpos = lax.broadcasted_iota(jnp.int32, (tkc, tq), 0)
    qpos = lax.broadcasted_iota(jnp.int32, (tkc, tq), 1)
    q0 = pl.multiple_of(qi * tq, tq)
    c0 = [cfb_ref[pl.ds(q0, 1), hh * LANES:(hh + 1) * LANES] for hh in range(2)]
    for hh in range(2):
        _softmax_init(m_sc.at[hh], l_sc.at[hh], acc_sc.at[hh])

    def chunk(c, diag):
        k0 = pl.multiple_of(c * tkc, tkc)
        kblk = k_ref[pl.ds(k0, tkc), :]
        if diag:
            visible = (kpos + k0) <= (qpos + q0)
        for hh in range(2):
            decay = c0[hh] - cfb_ref[pl.ds(k0, tkc), hh * LANES:(hh + 1) * LANES]
            sT = _dot(kblk, qms[hh]) + jnp.concatenate([decay] * (tq // LANES), axis=1)
            if diag:
                sT = jnp.where(visible, sT, MASKED)
            vTs = [v_ref[c * ch + j, hh * HEAD_DIM:(hh + 1) * HEAD_DIM, :] for j in range(ch)]
            _softmax_update_t([sT], [vTs], m_sc.at[hh], l_sc.at[hh], acc_sc.at[hh])

    n_full = _shr(qi, ch)

    def body(c, carry):
        chunk(c, False)
        return carry

    lax.fori_loop(0, n_full, body, 0)
    chunk(n_full, True)
    _write_pair(o_ref, l_sc, acc_sc)


def _softmax_scratch(tk):
    return [pltpu.VMEM((2, 1, tk), F32), pltpu.VMEM((2, 1, tk), F32),
            pltpu.VMEM((2, HEAD_DIM, tk), F32)]


def _fox_prompt_attention(qT, k, vTb, cfb):
    b, dq, t = qT.shape
    nkb, tk = vTb.shape[1], vTb.shape[3]
    npair = dq // LANES
    ch = FOX_CHUNK if nkb % FOX_CHUNK == 0 else 1
    return pl.pallas_call(
        functools.partial(_fox_prompt_body, tq=tk, ch=ch),
        grid=(b, npair, t // tk),
        in_specs=[pl.BlockSpec((None, LANES, tk), lambda i, p, j: (i, p, j)),
                  pl.BlockSpec((None, t, LANES), lambda i, p, j: (i, 0, p)),
                  pl.BlockSpec((None, nkb, LANES, tk), lambda i, p, j: (i, 0, p, 0)),
                  pl.BlockSpec((None, t, 2 * LANES), lambda i, p, j: (i, 0, p))],
        out_specs=pl.BlockSpec((None, LANES, tk), lambda i, p, j: (i, p, j)),
        out_shape=jax.ShapeDtypeStruct((b, dq, t), BF16),
        scratch_shapes=_softmax_scratch(tk),
        compiler_params=_params(("parallel", "parallel", "arbitrary")),
        name="fox_prompt_attention",
    )(qT, k, vTb, cfb)


def _alibi_slope(head_index_f32, n_heads):
    return jnp.exp2(-ALIBI_MAX_BIAS * (head_index_f32 + 1.0) / n_heads)


def _dil_prompt_body(q0_ref, q1_ref, q2_ref, k0_ref, k1_ref, k2_ref, v0_ref, v1_ref, v2_ref,
                     o_ref, m_sc, l_sc, acc_sc, *, tq, heads_per_group):
    hp = pl.program_id(1)
    qi = pl.program_id(2)
    q_refs, k_refs, v_refs = (q0_ref, q1_ref, q2_ref), (k0_ref, k1_ref, k2_ref), (v0_ref, v1_ref, v2_ref)
    krow = lax.broadcasted_iota(jnp.int32, (tq, tq), 0)
    qcol = lax.broadcasted_iota(jnp.int32, (tq, tq), 1)
    qk = qcol - krow
    for hh in range(2):
        _softmax_init(m_sc.at[hh], l_sc.at[hh], acc_sc.at[hh])
    for g, (win, dil) in enumerate(DIL_PATTERNS):
        qms = _pair_queries(q_refs[g][...])
        slopes = [LOG2E * _alibi_slope(jnp.full((1, tq), (g * heads_per_group + 2 * hp + hh).astype(F32), F32),
                                       N_GROUPS * heads_per_group) for hh in range(2)]
        n_off = (win + tq - 1) // tq + 1
        for first in range(0, n_off, DIL_CHUNK):
            sTs, vTs = ([], []), ([], [])
            for off in range(first, min(first + DIL_CHUNK, n_off)):
                kb = qi - off
                kbc = jnp.maximum(kb, 0)
                kblk = k_refs[g][pl.ds(pl.multiple_of(kbc * tq, tq), tq), :]
                dist = qk + off * tq
                ok = jnp.logical_and(jnp.logical_and(dist >= 0, dist <= win), (dist & (dil - 1)) == 0)
                ok = jnp.logical_and(ok, kb >= 0)
                distf = dist.astype(F32)
                for hh in range(2):
                    sTs[hh].append(jnp.where(ok, _dot(kblk, qms[hh]) - slopes[hh] * distf, MASKED))
                    vTs[hh].append([v_refs[g][kbc, hh * HEAD_DIM:(hh + 1) * HEAD_DIM, :]])
            for hh in range(2):
                _softmax_update_t(sTs[hh], vTs[hh], m_sc.at[hh], l_sc.at[hh], acc_sc.at[hh])
    _write_pair(o_ref, l_sc, acc_sc)


def _dil_prompt_attention(qT, k, vTb):
    b, dq, t = qT.shape
    nkb, tk = vTb.shape[1], vTb.shape[3]
    gw = dq // N_GROUPS
    npair = gw // LANES
    q_specs = [pl.BlockSpec((None, LANES, tk), lambda i, p, j, g=g: (i, g * npair + p, j))
               for g in range(N_GROUPS)]
    k_specs = [pl.BlockSpec((None, t, LANES), lambda i, p, j, g=g: (i, 0, g * npair + p))
               for g in range(N_GROUPS)]
    v_specs = [pl.BlockSpec((None, nkb, LANES, tk), lambda i, p, j, g=g: (i, 0, g * npair + p, 0))
               for g in range(N_GROUPS)]
    return pl.pallas_call(
        functools.partial(_dil_prompt_body, tq=tk, heads_per_group=gw // HEAD_DIM),
        grid=(b, npair, t // tk),
        in_specs=q_specs + k_specs + v_specs,
        out_specs=pl.BlockSpec((None, LANES, tk), lambda i, p, j: (i, p, j)),
        out_shape=jax.ShapeDtypeStruct((b, gw, t), BF16),
        scratch_shapes=_softmax_scratch(tk),
        compiler_params=_params(("parallel", "parallel", "arbitrary")),
        name="dil_prompt_attention",
    )(qT, qT, qT, k, k, k, vTb, vTb, vTb)


def _head_of_col(shape):
    return _shr(lax.broadcasted_iota(jnp.int32, shape, 1), HEAD_DIM)


def _block_diag_queries(q, nh):
    nq, width = q.shape
    keep = lax.broadcasted_iota(jnp.int32, (nh, width), 0) == _head_of_col((nh, width))
    rows = [jnp.where(keep, jnp.broadcast_to(q[i:i + 1, :], (nh, width)), 0.0) for i in range(nq)]
    return jnp.concatenate(rows, axis=0).astype(BF16)


def _block_diag_extract(acc, nq, nh):
    width = acc.shape[1]
    keep = lax.broadcasted_iota(jnp.int32, (nh, width), 0) == _head_of_col((nh, width))
    rows = [jnp.sum(jnp.where(keep, acc[i * nh:(i + 1) * nh, :], 0.0), axis=0, keepdims=True)
            for i in range(nq)]
    return jnp.concatenate(rows, axis=0)


def _own_token_mask(rows, nq, nh, b, inclusive):
    j = lax.broadcasted_iota(jnp.int32, (rows, PAGE), 1) - nq * b
    qrow = _shr(lax.broadcasted_iota(jnp.int32, (rows, PAGE), 0), nh)
    return jnp.logical_and(j >= 0, (j <= qrow) if inclusive else (j < qrow))


def _cat_pages(page_refs, half):
    return jnp.concatenate([r[half] for r in page_refs], axis=1)


def _sb_decode_body(pt_ref, q_ref, new_ref, *rest, nq, nh, pps):
    page_refs = rest[:pps]
    o_ref, qbd_sc, run_sc, acc_sc = rest[pps:]
    b = pl.program_id(0)
    p = pl.program_id(1)
    rows = nq * nh
    hd = nh * HEAD_DIM

    def block(kT, vT, mask):
        z = _dot(qbd_sc[...], kT.astype(BF16))
        sp = _softplus(z)
        l1 = -sp if mask is None else jnp.where(mask, -sp, 0.0)
        between = _dot(l1.astype(BF16), _tri(kT.shape[1], "gt")) + run_sc[...]
        w = jnp.exp((z - sp) + between)
        if mask is not None:
            w = jnp.where(mask, w, 0.0)
        acc_sc[...] += _nt(w.astype(BF16), vT.astype(BF16))
        run_sc[...] += jnp.sum(l1, axis=1, keepdims=True)

    @pl.when(p == 0)
    def _():
        qbd_sc[...] = _block_diag_queries(q_ref[...], nh)
        run_sc[...] = jnp.zeros_like(run_sc)
        acc_sc[...] = jnp.zeros_like(acc_sc)
        block(new_ref[0:hd, :], new_ref[hd:2 * hd, :], _own_token_mask(rows, nq, nh, b, False))

    @pl.when(jnp.logical_and(p > 0, jnp.max(run_sc[...]) > EXP_ZERO))
    def _():
        block(_cat_pages(page_refs, 0), _cat_pages(page_refs, 1), None)

    @pl.when(p == pl.num_programs(1) - 1)
    def _():
        o_ref[...] = _block_diag_extract(acc_sc[...], nq, nh)


def _page_specs(block_shape, layer, n_pages, pps):
    zeros = (0,) * (len(block_shape) - 2)

    def spec(i):
        def index(b, p, pt_ref):
            return (layer, pt_ref[b, n_pages - pps * jnp.maximum(p, 1) + i]) + zeros
        return pl.BlockSpec(block_shape, index)
    return [spec(i) for i in range(pps)]


def _sb_decode_attention(q, kvT_new, cacheT, layer, page_table):
    db, nq, hd = q.shape
    nh = hd // HEAD_DIM
    n_pages = page_table.shape[1]
    pps = PAGES_PER_STEP
    assert n_pages % pps == 0
    grid_spec = pltpu.PrefetchScalarGridSpec(
        num_scalar_prefetch=1, grid=(db, n_pages // pps + 1),
        in_specs=[pl.BlockSpec((None, nq, hd), lambda b, p, pt: (b, 0, 0)),
                  pl.BlockSpec((2 * hd, db * nq), lambda b, p, pt: (0, 0))]
        + _page_specs((None, None, 2, hd, PAGE), layer, n_pages, pps),
        out_specs=pl.BlockSpec((None, nq, hd), lambda b, p, pt: (b, 0, 0)),
        scratch_shapes=[pltpu.VMEM((nq * nh, hd), BF16), pltpu.VMEM((nq * nh, 1), F32),
                        pltpu.VMEM((nq * nh, hd), F32)])
    return pl.pallas_call(
        functools.partial(_sb_decode_body, nq=nq, nh=nh, pps=pps),
        grid_spec=grid_spec,
        out_shape=jax.ShapeDtypeStruct((db, nq, hd), F32),
        compiler_params=_params(("parallel", "arbitrary")),
        name="sb_decode_attention",
    )(page_table, q, kvT_new, *([cacheT] * pps))


def _softmax_block(s, vT, m_ref, l_ref, acc_ref):
    m_prev = m_ref[...]
    m_new = jnp.maximum(m_prev, jnp.max(s, axis=1, keepdims=True))
    alpha = jnp.exp(m_prev - m_new)
    p = jnp.exp(s - m_new)
    l_ref[...] = alpha * l_ref[...] + jnp.sum(p, axis=1, keepdims=True)
    acc_ref[...] = alpha * acc_ref[...] + _nt(p.astype(BF16), vT)
    m_ref[...] = m_new


def _fox_decode_body(pt_ref, q_ref, new_ref, lfnew_ref, *rest, nq, nh, pps):
    page_refs, lf_refs = rest[:pps], rest[pps:2 * pps]
    o_ref, qbd_sc, tail_sc, m_sc, l_sc, acc_sc = rest[2 * pps:]
    b = pl.program_id(0)
    p = pl.program_id(1)
    rows = nq * nh
    hd = nh * HEAD_DIM

    def block(kT, vT, lf, mask):
        later = _dot3_right(lf, _tri(lf.shape[1], "gt")) + tail_sc[...]
        s = _dot(qbd_sc[...], kT.astype(BF16)) + jnp.concatenate([later] * nq, axis=0)
        if mask is not None:
            s = jnp.where(mask, s, MASKED)
        _softmax_block(s, vT.astype(BF16), m_sc, l_sc, acc_sc)
        tail_sc[...] += jnp.sum(lf, axis=1, keepdims=True)

    @pl.when(p == 0)
    def _():
        qbd_sc[...] = _block_diag_queries(q_ref[...], nh)
        tail_sc[...] = jnp.zeros_like(tail_sc)
        _softmax_init(m_sc, l_sc, acc_sc)
        jl = lax.broadcasted_iota(jnp.int32, (nh, PAGE), 1) - nq * b
        own = jnp.logical_and(jl >= 0, jl < nq)
        block(new_ref[0:hd, :], new_ref[hd:2 * hd, :], jnp.where(own, lfnew_ref[...], 0.0),
              _own_token_mask(rows, nq, nh, b, True))

    @pl.when(p > 0)
    def _():
        block(_cat_pages(page_refs, 0), _cat_pages(page_refs, 1),
              jnp.concatenate([r[...] for r in lf_refs], axis=1), None)

    @pl.when(p == pl.num_programs(1) - 1)
    def _():
        o_ref[...] = _block_diag_extract(acc_sc[...] / l_sc[...], nq, nh)


def _fox_decode_attention(q, kvT_new, lfT_new, cacheT, logfT, layer, page_table):
    db, nq, hd = q.shape
    nh = hd // HEAD_DIM
    n_pages = page_table.shape[1]
    pps = PAGES_PER_STEP
    assert n_pages % pps == 0
    grid_spec = pltpu.PrefetchScalarGridSpec(
        num_scalar_prefetch=1, grid=(db, n_pages // pps + 1),
        in_specs=[pl.BlockSpec((None, nq, hd), lambda b, p, pt: (b, 0, 0)),
                  pl.BlockSpec((2 * hd, db * nq), lambda b, p, pt: (0, 0)),
                  pl.BlockSpec((nh, db * nq), lambda b, p, pt: (0, 0))]
        + _page_specs((None, None, 2, hd, PAGE), layer, n_pages, pps)
        + _page_specs((None, None, nh, PAGE), layer, n_pages, pps),
        out_specs=pl.BlockSpec((None, nq, hd), lambda b, p, pt: (b, 0, 0)),
        scratch_shapes=[pltpu.VMEM((nq * nh, hd), BF16), pltpu.VMEM((nh, 1), F32),
                        pltpu.VMEM((nq * nh, 1), F32), pltpu.VMEM((nq * nh, 1), F32),
                        pltpu.VMEM((nq * nh, hd), F32)])
    return pl.pallas_call(
        functools.partial(_fox_decode_body, nq=nq, nh=nh, pps=pps),
        grid_spec=grid_spec,
        out_shape=jax.ShapeDtypeStruct((db, nq, hd), F32),
        compiler_params=_params(("parallel", "arbitrary")),
        name="fox_decode_attention",
    )(page_table, q, kvT_new, lfT_new, *([cacheT] * pps), *([logfT] * pps))


def _dil_decode_body(q_ref, new_ref, s0_ref, s1_ref, s2_ref, o_ref, n0_ref, n1_ref, n2_ref,
                     *, nq, nh, past_len):
    b = pl.program_id(0)
    rows = nq * nh
    gw = nh * HEAD_DIM
    state_refs, out_refs = (s0_ref, s1_ref, s2_ref), (n0_ref, n1_ref, n2_ref)
    n_new = new_ref.shape[1]
    qrow_n = _shr(lax.broadcasted_iota(jnp.int32, (rows, n_new), 0), nh)
    j_new = lax.broadcasted_iota(jnp.int32, (rows, n_new), 1) - nq * b
    parts = []
    for g, (win, dil) in enumerate(DIL_PATTERNS):
        qbd = _block_diag_queries(q_ref[:, g * gw:(g + 1) * gw], nh)
        hrow = (lax.broadcasted_iota(jnp.int32, (rows, 1), 0) & (nh - 1)) + g * nh
        slope = _alibi_slope(hrow.astype(F32), N_GROUPS * nh)
        st_ref = state_refs[g]
        kT_new = new_ref[g * gw:(g + 1) * gw, :]
        vT_new = new_ref[(N_GROUPS + g) * gw:(N_GROUPS + g + 1) * gw, :]
        qrow = _shr(lax.broadcasted_iota(jnp.int32, (rows, win), 0), nh)
        c = lax.broadcasted_iota(jnp.int32, (rows, win), 1)
        dist = win + qrow - c
        ok = jnp.logical_and((dist & (dil - 1)) == 0, dist <= jnp.minimum(win, past_len + qrow))
        s_old = _dot(qbd, st_ref[0:gw, :].astype(BF16)) - slope * dist.astype(F32)
        s_old = jnp.where(ok, s_old, MASKED)
        dist_n = qrow_n - j_new
        ok_n = jnp.logical_and(jnp.logical_and(j_new >= 0, dist_n >= 0), (dist_n & (dil - 1)) == 0)
        s_new = _dot(qbd, kT_new.astype(BF16)) - slope * dist_n.astype(F32)
        s_new = jnp.where(ok_n, s_new, MASKED)
        m = jnp.maximum(jnp.max(s_old, axis=1, keepdims=True), jnp.max(s_new, axis=1, keepdims=True))
        e_old = jnp.exp(s_old - m)
        e_new = jnp.exp(s_new - m)
        l = jnp.sum(e_old, axis=1, keepdims=True) + jnp.sum(e_new, axis=1, keepdims=True)
        o = _nt(e_old.astype(BF16), st_ref[gw:2 * gw, :].astype(BF16)) + \
            _nt(e_new.astype(BF16), vT_new.astype(BF16))
        parts.append((m, l, o))
        lane = lax.broadcasted_iota(jnp.int32, (gw, PAGE), 1)
        shift = (PAGE - nq) - nq * b
        shift = jnp.where(shift < 0, shift + PAGE, shift)
        for half, new_rows in ((0, kT_new), (1, vT_new)):
            old = st_ref[half * gw:(half + 1) * gw, :]
            rolled = pltpu.roll(old, win - nq, 1)
            tail = jnp.where(lane >= PAGE - nq, pltpu.roll(new_rows, shift, 1), rolled[:, win - PAGE:])
            out_refs[g][half * gw:(half + 1) * gw, :] = rolled
            out_refs[g][half * gw:(half + 1) * gw, win - PAGE:win] = tail
    m_all = functools.reduce(jnp.maximum, [pt[0] for pt in parts])
    den = sum(jnp.exp(pt[0] - m_all) * pt[1] for pt in parts)
    num = sum(jnp.exp(pt[0] - m_all) * pt[2] for pt in parts)
    o_ref[...] = _block_diag_extract(num / den, nq, nh)


def _dil_decode_attention(q, kvT_new, states, past_len):
    db, nq, dq = q.shape
    gw = dq // N_GROUPS
    nh = gw // HEAD_DIM
    st_specs = [pl.BlockSpec((None, 2 * gw, w), lambda b: (b, 0, 0)) for w, _ in DIL_PATTERNS]
    return pl.pallas_call(
        functools.partial(_dil_decode_body, nq=nq, nh=nh, past_len=past_len),
        grid=(db,),
        in_specs=[pl.BlockSpec((None, nq, dq), lambda b: (b, 0, 0)),
                  pl.BlockSpec(kvT_new.shape, lambda b: (0, 0))] + st_specs,
        out_specs=[pl.BlockSpec((None, nq, gw), lambda b: (b, 0, 0))] + st_specs,
        out_shape=[jax.ShapeDtypeStruct((db, nq, gw), F32)] +
                  [jax.ShapeDtypeStruct(s.shape, F32) for s in states],
        compiler_params=_params(("parallel",)),
        name="dil_decode_attention",
    )(q, kvT_new, *states)


def _tokens_minor(x):
    lead = x.shape[:-4]
    t, a, bb, c = x.shape[-4:]
    n = len(lead)
    xt = jnp.transpose(x, tuple(range(n)) + (n + 1, n + 2, n + 3, n))
    return xt.reshape(lead + (a * bb * c, t))


def _tokens_major(xT, a, bb, c):
    lead = xT.shape[:-2]
    t = xT.shape[-1]
    n = len(lead)
    x = xT.reshape(lead + (a, bb, c, t))
    return jnp.transpose(x, tuple(range(n)) + (n + 3, n, n + 1, n + 2))


def kernel(x_prompt, x_sample, cache_sb_kv, cache_fox_kv, cache_fox_logf, state_dil_kv_w128, state_dil_kv_w512, state_dil_kv_w2048, page_table, g_mix_pre, g_mix_post, g_ffn_pre, g_ffn_post, sb_w_qkv, sb_w_o, fox_w_qkv, fox_w_f, fox_b_f, fox_w_o, dil_w_qkv, dil_w_o, w_ffn_up, w_ffn_down):
    b, t, d = x_prompt.shape
    db, nq, _ = x_sample.shape
    depth = g_mix_pre.shape[0]
    nh = d // HEAD_DIM
    hd = nh * HEAD_DIM
    ns = db * nq
    past_len = page_table.shape[1] * PAGE
    dil_states = (state_dil_kv_w128, state_dil_kv_w512, state_dil_kv_w2048)
    gheads = dil_states[0].shape[-2]
    gw = gheads * HEAD_DIM
    assert t % ATT_BLOCK == 0 and ns == PAGE
    rows_p = PROMPT_ROWS if t % PROMPT_ROWS == 0 else ATT_BLOCK
    ffn_rows = FFN_ROWS if (b * t) % FFN_ROWS == 0 else ATT_BLOCK

    sb_cacheT = _tokens_minor(cache_sb_kv).reshape(cache_sb_kv.shape[:2] + (2, hd, PAGE))
    fox_cacheT = _tokens_minor(cache_fox_kv).reshape(cache_fox_kv.shape[:2] + (2, hd, PAGE))
    fox_logfT = jnp.swapaxes(cache_fox_logf, 2, 3)
    dil_statesT = [_tokens_minor(s) for s in dil_states]

    yp = x_prompt
    ys = x_sample.reshape(1, ns, d)
    outs = {k: [] for k in ("sb_p", "sb_s", "fox_p", "fox_s", "fl_p", "fl_s")}
    dil_p = [[] for _ in DIL_PATTERNS]
    dil_s = [[] for _ in DIL_PATTERNS]
    slot = [0] * N_MIXERS
    for i in range(depth):
        kind = i % N_MIXERS
        li = slot[kind]
        slot[kind] += 1
        if kind == 2:
            w_qkv, w_o = dil_w_qkv[li], dil_w_o[li]
        elif kind == 1:
            w_qkv, w_o = fox_w_qkv[li], fox_w_o[li]
        else:
            w_qkv, w_o = sb_w_qkv[li], sb_w_o[li]
        nqf = w_qkv.shape[1] // 3
        w_o = w_o.astype(BF16)
        forget = (fox_w_f[li], fox_b_f[li]) if kind == 1 else None
        res_p = _qkv_proj(yp, g_mix_pre[i], w_qkv, ATT_BLOCK, "row" if kind == 0 else "col", forget)
        res_s = _qkv_proj(ys, g_mix_pre[i], w_qkv, ns, "row", forget)
        q_s = res_s[0].astype(F32).reshape(db, nq, nqf)
        kvT_s = res_s[1][0]
        if kind == 0:
            q_p, kvT_p, kvTb_p = res_p
            att_p = _sb_prompt_attention(q_p, kvTb_p)
            att_s = _sb_decode_attention(q_s, kvT_s, sb_cacheT, li, page_table)
            outs["sb_p"].append(_tokens_major(kvT_p, 2, nh, HEAD_DIM))
            outs["sb_s"].append(kvT_s.T.reshape(db, nq, 2, nh, HEAD_DIM))
        elif kind == 1:
            qT_p, k_p, kvT_p, vTb_p, lfT_p, cfb_p = res_p
            lfT_s = res_s[3][0]
            att_p = _fox_prompt_attention(qT_p, k_p, vTb_p, cfb_p)
            att_s = _fox_decode_attention(q_s, kvT_s, lfT_s, fox_cacheT, fox_logfT, li, page_table)
            outs["fox_p"].append(_tokens_major(kvT_p, 2, nh, HEAD_DIM))
            outs["fox_s"].append(kvT_s.T.reshape(db, nq, 2, nh, HEAD_DIM))
            outs["fl_p"].append(jnp.swapaxes(lfT_p, 1, 2))
            outs["fl_s"].append(lfT_s.T.reshape(db, nq, nh))
        else:
            assert t >= max(w for w, _ in DIL_PATTERNS)
            qT_p, k_p, kvT_p, vTb_p = res_p
            att_p = _dil_prompt_attention(qT_p, k_p, vTb_p)
            att_s, *new_states = _dil_decode_attention(
                q_s, kvT_s, [s[li] for s in dil_statesT], past_len)
            kv5 = kvT_p.reshape(b, 2, N_GROUPS, gw, t)
            for g, (win, _) in enumerate(DIL_PATTERNS):
                dil_p[g].append(_tokens_major(kv5[:, :, g, :, t - win:].reshape(b, 2 * gw, win),
                                              2, gheads, HEAD_DIM))
                dil_s[g].append(_tokens_major(new_states[g], 2, gheads, HEAD_DIM))
        att_s = att_s.reshape(1, ns, -1).astype(BF16)
        yp = _out_proj(att_p, w_o, g_mix_post[i], yp, rows_p, transposed=kind != 0)
        ys = _out_proj(att_s, w_o, g_mix_post[i], ys, ns)
        wup, wdn = w_ffn_up[i].astype(BF16), w_ffn_down[i].astype(BF16)
        yp = _ffn(yp.reshape(b * t, d), g_ffn_pre[i], wup, wdn, g_ffn_post[i],
                  ffn_rows, FFN_COLS).reshape(b, t, d)
        ys = _ffn(ys.reshape(ns, d), g_ffn_pre[i], wup, wdn, g_ffn_post[i],
                  ns, FFN_COLS).reshape(1, ns, d)
    return (yp, ys.reshape(db, nq, d),
            jnp.stack(outs["sb_p"]), jnp.stack(outs["sb_s"]),
            jnp.stack(outs["fox_p"]), jnp.stack(outs["fox_s"]),
            jnp.stack(outs["fl_p"]), jnp.stack(outs["fl_s"]),
            jnp.stack(dil_p[0]), jnp.stack(dil_s[0]),
            jnp.stack(dil_p[1]), jnp.stack(dil_s[1]),
            jnp.stack(dil_p[2]), jnp.stack(dil_s[2]))
```

```python
import functools

import jax
import jax.numpy as jnp
from jax import lax
from jax.experimental import pallas as pl
from jax.experimental.pallas import tpu as pltpu

F32 = jnp.float32
BF16 = jnp.bfloat16

HEAD_DIM = 64
LANES = 128
PAGE = 128
N_MIXERS = 3
DIL_PATTERNS = ((128, 1), (512, 4), (2048, 16))
N_GROUPS = len(DIL_PATTERNS)
RMS_EPS = 1e-6
ALIBI_MAX_BIAS = 8.0
Q_SCALE = HEAD_DIM ** -0.5
LOG2E = 1.4426950408889634
MASKED = -1e30
EXP_ZERO = -104.0
EXP2_ZERO = -160.0
NORM_SLACK = 1.02
ATT_BLOCK = 256
FOX_CHUNK = 4
DIL_CHUNK = 3
PAGES_PER_STEP = 4
PROMPT_ROWS = 512
FFN_ROWS = 1024
FFN_COLS = 1024
VMEM_LIMIT = 56 * 2 ** 20


def _params(sem):
    return pltpu.CompilerParams(dimension_semantics=sem, vmem_limit_bytes=VMEM_LIMIT)


def _nt(a, b):
    return lax.dot_general(a, b, (((1,), (1,)), ((), ())), preferred_element_type=F32)


def _dot(a, b):
    return jnp.dot(a, b, preferred_element_type=F32)


def _rms(x, g):
    return x * lax.rsqrt(jnp.mean(x * x, axis=-1, keepdims=True) + RMS_EPS) * g


def _softplus(z):
    return jnp.maximum(z, 0.0) + jnp.log(1.0 + jnp.exp(-jnp.abs(z)))


def _split3(x):
    a1 = x.astype(BF16)
    r1 = x - a1.astype(F32)
    a2 = r1.astype(BF16)
    a3 = (r1 - a2.astype(F32)).astype(BF16)
    return a1, a2, a3


def _dot3_right(x, m):
    a1, a2, a3 = _split3(x)
    return _dot(a1, m) + _dot(a2, m) + _dot(a3, m)


def _dot3_left(m, x):
    a1, a2, a3 = _split3(x)
    return _dot(m, a1) + _dot(m, a2) + _dot(m, a3)


def _tri(n, kind):
    r = lax.broadcasted_iota(jnp.int32, (n, n), 0)
    c = lax.broadcasted_iota(jnp.int32, (n, n), 1)
    m = (r > c) if kind == "gt" else (r >= c)
    return jnp.where(m, 1.0, 0.0).astype(BF16)


def _shr(x, n):
    assert n & (n - 1) == 0
    return jnp.right_shift(x, n.bit_length() - 1)


def _half_mask(shape, hh, axis):
    idx = lax.broadcasted_iota(jnp.int32, shape, axis)
    return (idx < HEAD_DIM) if hh == 0 else (idx >= HEAD_DIM)


def _qkv_body(*refs, layout, with_forget, chunk, sub):
    it = iter(refs)
    x_ref, g_ref, wq_ref = next(it), next(it), next(it)
    wk_ref = next(it) if layout == "col" else None
    wkvT_ref = next(it)
    if with_forget:
        wfT_ref, wf_ref, bcol_ref, brow_ref = next(it), next(it), next(it), next(it)
    q_ref = next(it)
    k_ref = next(it) if layout == "col" else None
    kvT_ref, tb_ref = next(it), next(it)
    if with_forget:
        lfT_ref, cfb_ref = next(it), next(it)
    h_sc = next(it)
    if with_forget:
        carry_sc = next(it)

    tm = x_ref.shape[0]
    h_sc[...] = _rms(x_ref[...], g_ref[...]).astype(BF16)
    nkv = wkvT_ref.shape[0]
    if layout == "row":
        for c in range(0, wq_ref.shape[1], chunk):
            q_ref[:, c:c + chunk] = (_dot(h_sc[...], wq_ref[:, c:c + chunk]) * Q_SCALE).astype(BF16)
        tb_rows = (0, nkv)
    else:
        for r in range(0, wq_ref.shape[0], chunk):
            q_ref[r:r + chunk, :] = (_nt(wq_ref[r:r + chunk, :], h_sc[...]) * (Q_SCALE * LOG2E)).astype(BF16)
        for c in range(0, wk_ref.shape[1], chunk):
            k_ref[:, c:c + chunk] = _dot(h_sc[...], wk_ref[:, c:c + chunk]).astype(BF16)
        tb_rows = (nkv // 2, nkv)
    for r in range(0, nkv, chunk):
        t = _nt(wkvT_ref[r:r + chunk, :], h_sc[...])
        kvT_ref[r:r + chunk, :] = t
        if r >= tb_rows[0]:
            tb = t.astype(BF16)
            for s in range(tm // sub):
                tb_ref[s, r - tb_rows[0]:r - tb_rows[0] + chunk, :] = tb[:, s * sub:(s + 1) * sub]
    if with_forget:
        nh = wfT_ref.shape[0]
        lfT_ref[...] = -_softplus(-(_nt(wfT_ref[...], h_sc[...]) + bcol_ref[...]))
        lf = -_softplus(-(_dot(h_sc[...], wf_ref[...]) + brow_ref[...]))

        @pl.when(pl.program_id(1) == 0)
        def _():
            carry_sc[...] = jnp.zeros_like(carry_sc)

        cf = _dot3_left(_tri(tm, "ge"), lf) + carry_sc[...]
        carry_sc[...] = cf[tm - 1:tm, :]
        rep = lax.broadcasted_iota(jnp.int32, (nh, nh * LANES), 0) == \
            _shr(lax.broadcasted_iota(jnp.int32, (nh, nh * LANES), 1), LANES)
        cfb_ref[...] = _dot3_right(cf * LOG2E, jnp.where(rep, 1.0, 0.0).astype(BF16))


def _qkv_proj(x, g, w_qkv, tm, layout, forget=None):
    b, t, d = x.shape
    n = w_qkv.shape[1] // 3
    sub = min(tm, ATT_BLOCK)
    nt = t // tm
    with_forget = forget is not None
    const = lambda i, j: (0, 0)
    args = [x, g.reshape(1, d)]
    in_specs = [pl.BlockSpec((None, tm, d), lambda i, j: (i, j, 0)), pl.BlockSpec((1, d), const)]
    if layout == "row":
        args += [w_qkv[:, :n].astype(BF16)]
        in_specs += [pl.BlockSpec((d, n), const)]
        out_shape = [jax.ShapeDtypeStruct((b, t, n), BF16)]
        out_specs = [pl.BlockSpec((None, tm, n), lambda i, j: (i, j, 0))]
        tb_rows = 2 * n
    else:
        args += [w_qkv[:, :n].T.astype(BF16), w_qkv[:, n:2 * n].astype(BF16)]
        in_specs += [pl.BlockSpec((n, d), const), pl.BlockSpec((d, n), const)]
        out_shape = [jax.ShapeDtypeStruct((b, n, t), BF16), jax.ShapeDtypeStruct((b, t, n), BF16)]
        out_specs = [pl.BlockSpec((None, n, tm), lambda i, j: (i, 0, j)),
                     pl.BlockSpec((None, tm, n), lambda i, j: (i, j, 0))]
        tb_rows = n
    args += [w_qkv[:, n:].T.astype(BF16)]
    in_specs += [pl.BlockSpec((2 * n, d), const)]
    out_shape += [jax.ShapeDtypeStruct((b, 2 * n, t), F32),
                  jax.ShapeDtypeStruct((b, t // sub, tb_rows, sub), BF16)]
    out_specs += [pl.BlockSpec((None, 2 * n, tm), lambda i, j: (i, 0, j)),
                  pl.BlockSpec((None, tm // sub, tb_rows, sub), lambda i, j: (i, j, 0, 0))]
    scratch = [pltpu.VMEM((tm, d), BF16)]
    if with_forget:
        w_f, b_f = forget
        nh = w_f.shape[1]
        args += [w_f.T.astype(BF16), w_f.astype(BF16), b_f.reshape(nh, 1), b_f.reshape(1, nh)]
        in_specs += [pl.BlockSpec((nh, d), const), pl.BlockSpec((d, nh), const),
                     pl.BlockSpec((nh, 1), const), pl.BlockSpec((1, nh), const)]
        out_shape += [jax.ShapeDtypeStruct((b, nh, t), F32),
                      jax.ShapeDtypeStruct((b, t, nh * LANES), F32)]
        out_specs += [pl.BlockSpec((None, nh, tm), lambda i, j: (i, 0, j)),
                      pl.BlockSpec((None, tm, nh * LANES), lambda i, j: (i, j, 0))]
        scratch += [pltpu.VMEM((1, nh), F32)]
    return pl.pallas_call(
        functools.partial(_qkv_body, layout=layout, with_forget=with_forget, chunk=512, sub=sub),
        grid=(b, nt), in_specs=in_specs, out_specs=out_specs, out_shape=out_shape,
        scratch_shapes=scratch,
        compiler_params=_params(("parallel", "arbitrary")),
        name="qkv_proj_" + layout + ("_forget" if with_forget else ""),
    )(*args)


def _oproj_body(a_ref, w_ref, g_ref, x_ref, y_ref, *, transposed):
    if transposed:
        m = lax.dot_general(a_ref[...], w_ref[...], (((0,), (0,)), ((), ())),
                            preferred_element_type=F32)
    else:
        m = _dot(a_ref[...], w_ref[...])
    y_ref[...] = x_ref[...] + _rms(m, g_ref[...])


def _out_proj(a, w, g, x, tm, transposed=False):
    b, t, d = x.shape
    k = w.shape[0]
    if transposed:
        a_spec = pl.BlockSpec((None, k, tm), lambda i, j: (i, 0, j))
    else:
        a_spec = pl.BlockSpec((None, tm, k), lambda i, j: (i, j, 0))
    return pl.pallas_call(
        functools.partial(_oproj_body, transposed=transposed), grid=(b, t // tm),
        in_specs=[a_spec,
                  pl.BlockSpec((k, d), lambda i, j: (0, 0)),
                  pl.BlockSpec((1, d), lambda i, j: (0, 0)),
                  pl.BlockSpec((None, tm, d), lambda i, j: (i, j, 0))],
        out_specs=pl.BlockSpec((None, tm, d), lambda i, j: (i, j, 0)),
        out_shape=jax.ShapeDtypeStruct((b, t, d), F32),
        compiler_params=_params(("parallel", "parallel")),
        name="out_proj_t" if transposed else "out_proj",
    )(a, w, g.reshape(1, d), x)


def _ffn_body(x_ref, gpre_ref, wup_ref, wdn_ref, gpost_ref, y_ref, h_sc, acc_sc):
    j = pl.program_id(1)

    @pl.when(j == 0)
    def _():
        h_sc[...] = _rms(x_ref[...], gpre_ref[...]).astype(BF16)
        acc_sc[...] = jnp.zeros_like(acc_sc)

    u = jnp.maximum(_dot(h_sc[...], wup_ref[...]), 0.0)
    acc_sc[...] += _dot((u * u).astype(BF16), wdn_ref[...])

    @pl.when(j == pl.num_programs(1) - 1)
    def _():
        y_ref[...] = x_ref[...] + _rms(acc_sc[...], gpost_ref[...])


def _ffn(x, gpre, wup, wdn, gpost, tm, tf):
    n, d = x.shape
    f = wup.shape[1]
    return pl.pallas_call(
        _ffn_body, grid=(n // tm, f // tf),
        in_specs=[pl.BlockSpec((tm, d), lambda i, j: (i, 0)),
                  pl.BlockSpec((1, d), lambda i, j: (0, 0)),
                  pl.BlockSpec((d, tf), lambda i, j: (0, j)),
                  pl.BlockSpec((tf, d), lambda i, j: (j, 0)),
                  pl.BlockSpec((1, d), lambda i, j: (0, 0))],
        out_specs=pl.BlockSpec((tm, d), lambda i, j: (i, 0)),
        out_shape=jax.ShapeDtypeStruct((n, d), F32),
        scratch_shapes=[pltpu.VMEM((tm, d), BF16), pltpu.VMEM((tm, d), F32)],
        compiler_params=_params(("parallel", "arbitrary")),
        name="ffn",
    )(x, gpre.reshape(1, d), wup, wdn, gpost.reshape(1, d))


def _pair_queries(qT):
    return [jnp.where(_half_mask(qT.shape, hh, 0), qT, jnp.zeros_like(qT)) for hh in range(2)]


def _sb_prompt_body(q_ref, k_ref, v_ref, o_ref, *, tq):
    qi = pl.program_id(2)
    q = q_ref[...]
    row = lax.broadcasted_iota(jnp.int32, (tq, tq), 0)
    col = lax.broadcasted_iota(jnp.int32, (tq, tq), 1)
    strictly_before = col < row
    after = _tri(tq, "gt")
    qms = [jnp.where(_half_mask(q.shape, hh, 1), q, jnp.zeros_like(q)) for hh in range(2)]

    def block(kb, state, diag):
        kT, vT = k_ref[kb], v_ref[kb]
        new_state = []
        for hh in range(2):
            run, acc = state[2 * hh], state[2 * hh + 1]
            z = _dot(qms[hh], kT)
            sp = _softplus(z)
            l1 = jnp.where(strictly_before, -sp, 0.0) if diag else -sp
            between = _dot(l1.astype(BF16), after) + run
            w = jnp.exp((z - sp) + between)
            if diag:
                w = jnp.where(strictly_before, w, 0.0)
            new_state += [run + jnp.sum(l1, axis=1, keepdims=True),
                          acc + _nt(w.astype(BF16), vT)]
        return tuple(new_state)

    def highest_run(state):
        return jnp.maximum(jnp.max(state[0]), jnp.max(state[2]))

    zeros = (jnp.zeros((tq, 1), F32), jnp.zeros((tq, LANES), F32))
    state0 = block(qi, zeros + zeros, True)

    def cond(c):
        return jnp.logical_and(c[0] >= 0, c[1] > EXP_ZERO)

    def body(c):
        state = block(c[0], c[2:], False)
        return (c[0] - 1, highest_run(state)) + state

    final = lax.while_loop(cond, body, (qi - 1, highest_run(state0)) + state0)
    o_ref[...] = jnp.where(_half_mask((tq, LANES), 0, 1), final[3], final[5]).astype(o_ref.dtype)


def _sb_prompt_attention(q, kvTb):
    b, t, dq = q.shape
    nkb, tk = kvTb.shape[1], kvTb.shape[3]
    npair = dq // LANES
    return pl.pallas_call(
        functools.partial(_sb_prompt_body, tq=tk),
        grid=(b, npair, t // tk),
        in_specs=[pl.BlockSpec((None, tk, LANES), lambda i, p, j: (i, j, p)),
                  pl.BlockSpec((None, nkb, LANES, tk), lambda i, p, j: (i, 0, p, 0)),
                  pl.BlockSpec((None, nkb, LANES, tk), lambda i, p, j: (i, 0, npair + p, 0))],
        out_specs=pl.BlockSpec((None, tk, LANES), lambda i, p, j: (i, j, p)),
        out_shape=jax.ShapeDtypeStruct((b, t, dq), BF16),
        compiler_params=_params(("parallel", "parallel", "arbitrary")),
        name="sb_prompt_attention",
    )(q, kvTb, kvTb)


def _softmax_update_t(sTs, vTs, m_ref, l_ref, acc_ref):
    m_prev = m_ref[...]
    m_new = m_prev
    for sT in sTs:
        m_new = jnp.maximum(m_new, jnp.max(sT, axis=0, keepdims=True))
    alpha = jnp.exp2(m_prev - m_new)
    l = alpha * l_ref[...]
    acc = alpha * acc_ref[...]
    for sT, vT_list in zip(sTs, vTs):
        p = jnp.exp2(sT - m_new)
        l = l + jnp.sum(p, axis=0, keepdims=True)
        pb = p.astype(BF16)
        tk = vT_list[0].shape[1]
        for j, vT in enumerate(vT_list):
            acc = acc + _dot(vT, pb[j * tk:(j + 1) * tk, :])
    l_ref[...] = l
    acc_ref[...] = acc
    m_ref[...] = m_new


def _softmax_init(m_ref, l_ref, acc_ref):
    m_ref[...] = jnp.full(m_ref.shape, MASKED, F32)
    l_ref[...] = jnp.zeros(l_ref.shape, F32)
    acc_ref[...] = jnp.zeros(acc_ref.shape, F32)


def _write_pair(o_ref, l_sc, acc_sc):
    for hh in range(2):
        o_ref[hh * HEAD_DIM:(hh + 1) * HEAD_DIM, :] = (acc_sc[hh] / l_sc[hh]).astype(o_ref.dtype)


def _fox_prompt_body(qT_ref, k_ref, v_ref, cfb_ref, o_ref, m_sc, l_sc, acc_sc, knorm_sc, *, tq, ch):
    qi = pl.program_id(2)
    qms = _pair_queries(qT_ref[...])
    tkc = ch * tq
    kpos = lax.broadcasted_iota(jnp.int32, (tkc, tq), 0)
    qpos = lax.broadcasted_iota(jnp.int32, (tkc, tq), 1)
    q0 = pl.multiple_of(qi * tq, tq)
    c0 = [cfb_ref[pl.ds(q0, 1), hh * LANES:(hh + 1) * LANES] for hh in range(2)]
    for hh in range(2):
        _softmax_init(m_sc.at[hh], l_sc.at[hh], acc_sc.at[hh])

    def chunk(c, diag):
        k0 = pl.multiple_of(c * tkc, tkc)
        kblk = k_ref[pl.ds(k0, tkc), :]
        if diag:
            visible = (kpos + k0) <= (qpos + q0)
        for hh in range(2):
            decay = c0[hh] - cfb_ref[pl.ds(k0, tkc), hh * LANES:(hh + 1) * LANES]
            sT = _dot(kblk, qms[hh]) + jnp.concatenate([decay] * (tq // LANES), axis=1)
            if diag:
                sT = jnp.where(visible, sT, MASKED)
            vTs = [v_ref[c * ch + j, hh * HEAD_DIM:(hh + 1) * HEAD_DIM, :] for j in range(ch)]
            _softmax_update_t([sT], [vTs], m_sc.at[hh], l_sc.at[hh], acc_sc.at[hh])

    @pl.when(qi == 0)
    def _():
        same_head = _half_mask((LANES, LANES), 0, 0) == _half_mask((LANES, LANES), 0, 1)
        ones = jnp.where(same_head, 1.0, 0.0).astype(BF16)
        best = jnp.zeros((1, LANES), F32)
        for r in range(0, k_ref.shape[0], tkc):
            kk = k_ref[r:r + tkc, :].astype(F32)
            sq = _dot((kk * kk).astype(BF16), ones)
            best = jnp.maximum(best, jnp.max(sq, axis=0, keepdims=True))
        knorm_sc[...] = best

    qf = [qm.astype(F32) for qm in qms]
    qk_bound = [jnp.sqrt(jnp.max(jnp.sum(qf[hh] * qf[hh], axis=0, keepdims=True))
                         * jnp.max(jnp.where(_half_mask((1, LANES), hh, 1), knorm_sc[...], 0.0)))
                * NORM_SLACK for hh in range(2)]

    n_full = _shr(qi, ch)
    chunk(n_full, True)

    def live(c):
        tail = pl.multiple_of(jnp.maximum(c, 0) * tkc + (tkc - 8), 8)
        reach = []
        for hh in range(2):
            newest = cfb_ref[pl.ds(tail, 8), hh * LANES:(hh + 1) * LANES][7:8, :]
            decay = jnp.max(c0[hh] - newest)
            reach.append(qk_bound[hh] + decay - jnp.min(m_sc[hh]))
        return jnp.maximum(reach[0], reach[1]) > EXP2_ZERO

    def cond(carry):
        return jnp.logical_and(carry[0] >= 0, carry[1])

    def body(carry):
        chunk(carry[0], False)
        return carry[0] - 1, live(carry[0] - 1)

    lax.while_loop(cond, body, (n_full - 1, live(n_full - 1)))
    _write_pair(o_ref, l_sc, acc_sc)


def _softmax_scratch(tk):
    return [pltpu.VMEM((2, 1, tk), F32), pltpu.VMEM((2, 1, tk), F32),
            pltpu.VMEM((2, HEAD_DIM, tk), F32)]


def _fox_prompt_attention(qT, k, vTb, cfb):
    b, dq, t = qT.shape
    nkb, tk = vTb.shape[1], vTb.shape[3]
    npair = dq // LANES
    ch = FOX_CHUNK if nkb % FOX_CHUNK == 0 else 1
    return pl.pallas_call(
        functools.partial(_fox_prompt_body, tq=tk, ch=ch),
        grid=(b, npair, t // tk),
        in_specs=[pl.BlockSpec((None, LANES, tk), lambda i, p, j: (i, p, j)),
                  pl.BlockSpec((None, t, LANES), lambda i, p, j: (i, 0, p)),
                  pl.BlockSpec((None, nkb, LANES, tk), lambda i, p, j: (i, 0, p, 0)),
                  pl.BlockSpec((None, t, 2 * LANES), lambda i, p, j: (i, 0, p))],
        out_specs=pl.BlockSpec((None, LANES, tk), lambda i, p, j: (i, p, j)),
        out_shape=jax.ShapeDtypeStruct((b, dq, t), BF16),
        scratch_shapes=_softmax_scratch(tk) + [pltpu.VMEM((1, LANES), F32)],
        compiler_params=_params(("parallel", "parallel", "arbitrary")),
        name="fox_prompt_attention",
    )(qT, k, vTb, cfb)


def _alibi_slope(head_index_f32, n_heads):
    return jnp.exp2(-ALIBI_MAX_BIAS * (head_index_f32 + 1.0) / n_heads)


def _dil_prompt_body(q0_ref, q1_ref, q2_ref, k0_ref, k1_ref, k2_ref, v0_ref, v1_ref, v2_ref,
                     o_ref, m_sc, l_sc, acc_sc, *, tq, heads_per_group):
    hp = pl.program_id(1)
    qi = pl.program_id(2)
    q_refs, k_refs, v_refs = (q0_ref, q1_ref, q2_ref), (k0_ref, k1_ref, k2_ref), (v0_ref, v1_ref, v2_ref)
    krow = lax.broadcasted_iota(jnp.int32, (tq, tq), 0)
    qcol = lax.broadcasted_iota(jnp.int32, (tq, tq), 1)
    qk = qcol - krow
    for hh in range(2):
        _softmax_init(m_sc.at[hh], l_sc.at[hh], acc_sc.at[hh])
    for g, (win, dil) in enumerate(DIL_PATTERNS):
        qms = _pair_queries(q_refs[g][...])
        slopes = [LOG2E * _alibi_slope(jnp.full((1, tq), (g * heads_per_group + 2 * hp + hh).astype(F32), F32),
                                       N_GROUPS * heads_per_group) for hh in range(2)]
        n_off = (win + tq - 1) // tq + 1
        for first in range(0, n_off, DIL_CHUNK):
            sTs, vTs = ([], []), ([], [])
            for off in range(first, min(first + DIL_CHUNK, n_off)):
                kb = qi - off
                kbc = jnp.maximum(kb, 0)
                kblk = k_refs[g][pl.ds(pl.multiple_of(kbc * tq, tq), tq), :]
                dist = qk + off * tq
                ok = jnp.logical_and(jnp.logical_and(dist >= 0, dist <= win), (dist & (dil - 1)) == 0)
                ok = jnp.logical_and(ok, kb >= 0)
                distf = dist.astype(F32)
                for hh in range(2):
                    sTs[hh].append(jnp.where(ok, _dot(kblk, qms[hh]) - slopes[hh] * distf, MASKED))
                    vTs[hh].append([v_refs[g][kbc, hh * HEAD_DIM:(hh + 1) * HEAD_DIM, :]])
            for hh in range(2):
                _softmax_update_t(sTs[hh], vTs[hh], m_sc.at[hh], l_sc.at[hh], acc_sc.at[hh])
    _write_pair(o_ref, l_sc, acc_sc)


def _dil_prompt_attention(qT, k, vTb):
    b, dq, t = qT.shape
    nkb, tk = vTb.shape[1], vTb.shape[3]
    gw = dq // N_GROUPS
    npair = gw // LANES
    q_specs = [pl.BlockSpec((None, LANES, tk), lambda i, p, j, g=g: (i, g * npair + p, j))
               for g in range(N_GROUPS)]
    k_specs = [pl.BlockSpec((None, t, LANES), lambda i, p, j, g=g: (i, 0, g * npair + p))
               for g in range(N_GROUPS)]
    v_specs = [pl.BlockSpec((None, nkb, LANES, tk), lambda i, p, j, g=g: (i, 0, g * npair + p, 0))
               for g in range(N_GROUPS)]
    return pl.pallas_call(
        functools.partial(_dil_prompt_body, tq=tk, heads_per_group=gw // HEAD_DIM),
        grid=(b, npair, t // tk),
        in_specs=q_specs + k_specs + v_specs,
        out_specs=pl.BlockSpec((None, LANES, tk), lambda i, p, j: (i, p, j)),
        out_shape=jax.ShapeDtypeStruct((b, gw, t), BF16),
        scratch_shapes=_softmax_scratch(tk),
        compiler_params=_params(("parallel", "parallel", "arbitrary")),
        name="dil_prompt_attention",
    )(qT, qT, qT, k, k, k, vTb, vTb, vTb)


def _head_of_col(shape):
    return _shr(lax.broadcasted_iota(jnp.int32, shape, 1), HEAD_DIM)


def _block_diag_queries(q, nh):
    nq, width = q.shape
    keep = lax.broadcasted_iota(jnp.int32, (nh, width), 0) == _head_of_col((nh, width))
    rows = [jnp.where(keep, jnp.broadcast_to(q[i:i + 1, :], (nh, width)), 0.0) for i in range(nq)]
    return jnp.concatenate(rows, axis=0).astype(BF16)


def _block_diag_extract(acc, nq, nh):
    width = acc.shape[1]
    keep = lax.broadcasted_iota(jnp.int32, (nh, width), 0) == _head_of_col((nh, width))
    rows = [jnp.sum(jnp.where(keep, acc[i * nh:(i + 1) * nh, :], 0.0), axis=0, keepdims=True)
            for i in range(nq)]
    return jnp.concatenate(rows, axis=0)


def _own_token_mask(rows, nq, nh, b, inclusive):
    j = lax.broadcasted_iota(jnp.int32, (rows, PAGE), 1) - nq * b
    qrow = _shr(lax.broadcasted_iota(jnp.int32, (rows, PAGE), 0), nh)
    return jnp.logical_and(j >= 0, (j <= qrow) if inclusive else (j < qrow))


def _cat_pages(page_refs, half):
    return jnp.concatenate([r[half] for r in page_refs], axis=1)


def _sb_decode_block(qbd, kT, vT, mask, run, acc):
    z = _dot(qbd, kT.astype(BF16))
    sp = _softplus(z)
    l1 = -sp if mask is None else jnp.where(mask, -sp, 0.0)
    between = _dot(l1.astype(BF16), _tri(kT.shape[1], "gt")) + run
    w = jnp.exp((z - sp) + between)
    if mask is not None:
        w = jnp.where(mask, w, 0.0)
    return run + jnp.sum(l1, axis=1, keepdims=True), acc + _nt(w.astype(BF16), vT.astype(BF16))


def _sb_decode_head_body(pt_ref, q_ref, new_ref, *rest, nq, nh, pps):
    page_refs = rest[:pps]
    acc_ref, run_ref, qbd_sc, run_sc, acc_sc = rest[pps:]
    b = pl.program_id(0)
    p = pl.program_id(1)
    rows = nq * nh
    hd = nh * HEAD_DIM

    @pl.when(p == 0)
    def _():
        qbd_sc[...] = _block_diag_queries(q_ref[...], nh)
        run, acc = _sb_decode_block(qbd_sc[...], new_ref[0:hd, :], new_ref[hd:2 * hd, :],
                                    _own_token_mask(rows, nq, nh, b, False),
                                    jnp.zeros(run_sc.shape, F32), jnp.zeros(acc_sc.shape, F32))
        run_sc[...] = run
        acc_sc[...] = acc

    @pl.when(p == 1)
    def _():
        run, acc = _sb_decode_block(qbd_sc[...], _cat_pages(page_refs, 0), _cat_pages(page_refs, 1),
                                    None, run_sc[...], acc_sc[...])
        acc_ref[...] = acc
        run_ref[...] = jnp.broadcast_to(run, run_ref.shape)


def _sb_decode_tail_body(pt_ref, done_ref, q_ref, acc_in_ref, run_in_ref, *rest, nq, nh, pps):
    page_refs = rest[:pps]
    o_ref, qbd_sc, run_sc, acc_sc = rest[pps:]
    b = pl.program_id(0)
    p = pl.program_id(1)

    @pl.when(p == 0)
    def _():
        qbd_sc[...] = _block_diag_queries(q_ref[...], nh)
        run_sc[...] = run_in_ref[:, 0:1]
        acc_sc[...] = acc_in_ref[...]

    @pl.when(jnp.logical_and(done_ref[b] == 0, jnp.max(run_sc[...]) > EXP_ZERO))
    def _():
        run, acc = _sb_decode_block(qbd_sc[...], _cat_pages(page_refs, 0), _cat_pages(page_refs, 1),
                                    None, run_sc[...], acc_sc[...])
        run_sc[...] = run
        acc_sc[...] = acc

    @pl.when(p == pl.num_programs(1) - 1)
    def _():
        o_ref[...] = _block_diag_extract(acc_sc[...], nq, nh)


def _page_specs(block_shape, layer, n_pages, pps, group_of):
    zeros = (0,) * (len(block_shape) - 2)

    def spec(i):
        def index(b, p, pt_ref, *prefetch):
            return (layer, pt_ref[b, n_pages - pps * group_of(b, p, *prefetch) + i]) + zeros
        return pl.BlockSpec(block_shape, index)
    return [spec(i) for i in range(pps)]


def _sb_decode_attention(q, kvT_new, cacheT, layer, page_table):
    db, nq, hd = q.shape
    nh = hd // HEAD_DIM
    rows = nq * nh
    n_pages = page_table.shape[1]
    pps = PAGES_PER_STEP
    n_groups = n_pages // pps
    assert n_pages % pps == 0 and n_groups >= 2
    page_block = (None, None, 2, hd, PAGE)
    scratch = [pltpu.VMEM((rows, hd), BF16), pltpu.VMEM((rows, 1), F32), pltpu.VMEM((rows, hd), F32)]
    state_specs = [pl.BlockSpec((None, rows, hd), lambda b, p, *_: (b, 0, 0)),
                   pl.BlockSpec((None, rows, LANES), lambda b, p, *_: (b, 0, 0))]
    q_spec = pl.BlockSpec((None, nq, hd), lambda b, p, *_: (b, 0, 0))
    acc, run = pl.pallas_call(
        functools.partial(_sb_decode_head_body, nq=nq, nh=nh, pps=pps),
        grid_spec=pltpu.PrefetchScalarGridSpec(
            num_scalar_prefetch=1, grid=(db, 2),
            in_specs=[q_spec, pl.BlockSpec((2 * hd, db * nq), lambda b, p, pt: (0, 0))]
            + _page_specs(page_block, layer, n_pages, pps, lambda b, p: 1),
            out_specs=state_specs, scratch_shapes=scratch),
        out_shape=[jax.ShapeDtypeStruct((db, rows, hd), F32),
                   jax.ShapeDtypeStruct((db, rows, LANES), F32)],
        compiler_params=_params(("parallel", "arbitrary")),
        name="sb_decode_head",
    )(page_table, q, kvT_new, *([cacheT] * pps))
    done = (jnp.max(run[:, :, 0], axis=1) <= EXP_ZERO).astype(jnp.int32)
    return pl.pallas_call(
        functools.partial(_sb_decode_tail_body, nq=nq, nh=nh, pps=pps),
        grid_spec=pltpu.PrefetchScalarGridSpec(
            num_scalar_prefetch=2, grid=(db, n_groups - 1),
            in_specs=[q_spec] + state_specs
            + _page_specs(page_block, layer, n_pages, pps,
                          lambda b, p, done_ref: jnp.where(done_ref[b] != 0, 1, p + 2)),
            out_specs=pl.BlockSpec((None, nq, hd), lambda b, p, *_: (b, 0, 0)),
            scratch_shapes=scratch),
        out_shape=jax.ShapeDtypeStruct((db, nq, hd), F32),
        compiler_params=_params(("parallel", "arbitrary")),
        name="sb_decode_tail",
    )(page_table, done, q, acc, run, *([cacheT] * pps))


def _softmax_block(s, vT, m_ref, l_ref, acc_ref):
    m_prev = m_ref[...]
    m_new = jnp.maximum(m_prev, jnp.max(s, axis=1, keepdims=True))
    alpha = jnp.exp(m_prev - m_new)
    p = jnp.exp(s - m_new)
    l_ref[...] = alpha * l_ref[...] + jnp.sum(p, axis=1, keepdims=True)
    acc_ref[...] = alpha * acc_ref[...] + _nt(p.astype(BF16), vT)
    m_ref[...] = m_new


def _fox_decode_body(pt_ref, q_ref, new_ref, lfnew_ref, *rest, nq, nh, pps):
    page_refs, lf_refs = rest[:pps], rest[pps:2 * pps]
    o_ref, qbd_sc, tail_sc, m_sc, l_sc, acc_sc = rest[2 * pps:]
    b = pl.program_id(0)
    p = pl.program_id(1)
    rows = nq * nh
    hd = nh * HEAD_DIM

    def block(kT, vT, lf, mask):
        later = _dot3_right(lf, _tri(lf.shape[1], "gt")) + tail_sc[...]
        s = _dot(qbd_sc[...], kT.astype(BF16)) + jnp.concatenate([later] * nq, axis=0)
        if mask is not None:
            s = jnp.where(mask, s, MASKED)
        _softmax_block(s, vT.astype(BF16), m_sc, l_sc, acc_sc)
        tail_sc[...] += jnp.sum(lf, axis=1, keepdims=True)

    @pl.when(p == 0)
    def _():
        qbd_sc[...] = _block_diag_queries(q_ref[...], nh)
        tail_sc[...] = jnp.zeros_like(tail_sc)
        _softmax_init(m_sc, l_sc, acc_sc)
        jl = lax.broadcasted_iota(jnp.int32, (nh, PAGE), 1) - nq * b
        own = jnp.logical_and(jl >= 0, jl < nq)
        block(new_ref[0:hd, :], new_ref[hd:2 * hd, :], jnp.where(own, lfnew_ref[...], 0.0),
              _own_token_mask(rows, nq, nh, b, True))

    @pl.when(p > 0)
    def _():
        block(_cat_pages(page_refs, 0), _cat_pages(page_refs, 1),
              jnp.concatenate([r[...] for r in lf_refs], axis=1), None)

    @pl.when(p == pl.num_programs(1) - 1)
    def _():
        o_ref[...] = _block_diag_extract(acc_sc[...] / l_sc[...], nq, nh)


def _fox_decode_attention(q, kvT_new, lfT_new, cacheT, logfT, layer, page_table):
    db, nq, hd = q.shape
    nh = hd // HEAD_DIM
    n_pages = page_table.shape[1]
    pps = PAGES_PER_STEP
    assert n_pages % pps == 0
    group_of = lambda b, p: jnp.maximum(p, 1)
    grid_spec = pltpu.PrefetchScalarGridSpec(
        num_scalar_prefetch=1, grid=(db, n_pages // pps + 1),
        in_specs=[pl.BlockSpec((None, nq, hd), lambda b, p, pt: (b, 0, 0)),
                  pl.BlockSpec((2 * hd, db * nq), lambda b, p, pt: (0, 0)),
                  pl.BlockSpec((nh, db * nq), lambda b, p, pt: (0, 0))]
        + _page_specs((None, None, 2, hd, PAGE), layer, n_pages, pps, group_of)
        + _page_specs((None, None, nh, PAGE), layer, n_pages, pps, group_of),
        out_specs=pl.BlockSpec((None, nq, hd), lambda b, p, pt: (b, 0, 0)),
        scratch_shapes=[pltpu.VMEM((nq * nh, hd), BF16), pltpu.VMEM((nh, 1), F32),
                        pltpu.VMEM((nq * nh, 1), F32), pltpu.VMEM((nq * nh, 1), F32),
                        pltpu.VMEM((nq * nh, hd), F32)])
    return pl.pallas_call(
        functools.partial(_fox_decode_body, nq=nq, nh=nh, pps=pps),
        grid_spec=grid_spec,
        out_shape=jax.ShapeDtypeStruct((db, nq, hd), F32),
        compiler_params=_params(("parallel", "arbitrary")),
        name="fox_decode_attention",
    )(page_table, q, kvT_new, lfT_new, *([cacheT] * pps), *([logfT] * pps))


def _dil_decode_body(q_ref, new_ref, s0_ref, s1_ref, s2_ref, o_ref, n0_ref, n1_ref, n2_ref,
                     *, nq, nh, past_len):
    b = pl.program_id(0)
    rows = nq * nh
    gw = nh * HEAD_DIM
    state_refs, out_refs = (s0_ref, s1_ref, s2_ref), (n0_ref, n1_ref, n2_ref)
    n_new = new_ref.shape[1]
    qrow_n = _shr(lax.broadcasted_iota(jnp.int32, (rows, n_new), 0), nh)
    j_new = lax.broadcasted_iota(jnp.int32, (rows, n_new), 1) - nq * b
    parts = []
    for g, (win, dil) in enumerate(DIL_PATTERNS):
        qbd = _block_diag_queries(q_ref[:, g * gw:(g + 1) * gw], nh)
        hrow = (lax.broadcasted_iota(jnp.int32, (rows, 1), 0) & (nh - 1)) + g * nh
        slope = _alibi_slope(hrow.astype(F32), N_GROUPS * nh)
        st_ref = state_refs[g]
        kT_new = new_ref[g * gw:(g + 1) * gw, :]
        vT_new = new_ref[(N_GROUPS + g) * gw:(N_GROUPS + g + 1) * gw, :]
        qrow = _shr(lax.broadcasted_iota(jnp.int32, (rows, win), 0), nh)
        c = lax.broadcasted_iota(jnp.int32, (rows, win), 1)
        dist = win + qrow - c
        ok = jnp.logical_and((dist & (dil - 1)) == 0, dist <= jnp.minimum(win, past_len + qrow))
        s_old = _dot(qbd, st_ref[0:gw, :].astype(BF16)) - slope * dist.astype(F32)
        s_old = jnp.where(ok, s_old, MASKED)
        dist_n = qrow_n - j_new
        ok_n = jnp.logical_and(jnp.logical_and(j_new >= 0, dist_n >= 0), (dist_n & (dil - 1)) == 0)
        s_new = _dot(qbd, kT_new.astype(BF16)) - slope * dist_n.astype(F32)
        s_new = jnp.where(ok_n, s_new, MASKED)
        m = jnp.maximum(jnp.max(s_old, axis=1, keepdims=True), jnp.max(s_new, axis=1, keepdims=True))
        e_old = jnp.exp(s_old - m)
        e_new = jnp.exp(s_new - m)
        l = jnp.sum(e_old, axis=1, keepdims=True) + jnp.sum(e_new, axis=1, keepdims=True)
        o = _nt(e_old.astype(BF16), st_ref[gw:2 * gw, :].astype(BF16)) + \
            _nt(e_new.astype(BF16), vT_new.astype(BF16))
        parts.append((m, l, o))
        lane = lax.broadcasted_iota(jnp.int32, (gw, PAGE), 1)
        shift = (PAGE - nq) - nq * b
        shift = jnp.where(shift < 0, shift + PAGE, shift)
        for half, new_rows in ((0, kT_new), (1, vT_new)):
            old = st_ref[half * gw:(half + 1) * gw, :]
            rolled = pltpu.roll(old, win - nq, 1)
            tail = jnp.where(lane >= PAGE - nq, pltpu.roll(new_rows, shift, 1), rolled[:, win - PAGE:])
            out_refs[g][half * gw:(half + 1) * gw, :] = rolled
            out_refs[g][half * gw:(half + 1) * gw, win - PAGE:win] = tail
    m_all = functools.reduce(jnp.maximum, [pt[0] for pt in parts])
    den = sum(jnp.exp(pt[0] - m_all) * pt[1] for pt in parts)
    num = sum(jnp.exp(pt[0] - m_all) * pt[2] for pt in parts)
    o_ref[...] = _block_diag_extract(num / den, nq, nh)


def _dil_decode_attention(q, kvT_new, states, past_len):
    db, nq, dq = q.shape
    gw = dq // N_GROUPS
    nh = gw // HEAD_DIM
    st_specs = [pl.BlockSpec((None, 2 * gw, w), lambda b: (b, 0, 0)) for w, _ in DIL_PATTERNS]
    return pl.pallas_call(
        functools.partial(_dil_decode_body, nq=nq, nh=nh, past_len=past_len),
        grid=(db,),
        in_specs=[pl.BlockSpec((None, nq, dq), lambda b: (b, 0, 0)),
                  pl.BlockSpec(kvT_new.shape, lambda b: (0, 0))] + st_specs,
        out_specs=[pl.BlockSpec((None, nq, gw), lambda b: (b, 0, 0))] + st_specs,
        out_shape=[jax.ShapeDtypeStruct((db, nq, gw), F32)] +
                  [jax.ShapeDtypeStruct(s.shape, F32) for s in states],
        compiler_params=_params(("parallel",)),
        name="dil_decode_attention",
    )(q, kvT_new, *states)


def _tokens_minor(x):
    lead = x.shape[:-4]
    t, a, bb, c = x.shape[-4:]
    n = len(lead)
    xt = jnp.transpose(x, tuple(range(n)) + (n + 1, n + 2, n + 3, n))
    return xt.reshape(lead + (a * bb * c, t))


def _tokens_major(xT, a, bb, c):
    lead = xT.shape[:-2]
    t = xT.shape[-1]
    n = len(lead)
    x = xT.reshape(lead + (a, bb, c, t))
    return jnp.transpose(x, tuple(range(n)) + (n + 3, n, n + 1, n + 2))


def kernel(x_prompt, x_sample, cache_sb_kv, cache_fox_kv, cache_fox_logf, state_dil_kv_w128, state_dil_kv_w512, state_dil_kv_w2048, page_table, g_mix_pre, g_mix_post, g_ffn_pre, g_ffn_post, sb_w_qkv, sb_w_o, fox_w_qkv, fox_w_f, fox_b_f, fox_w_o, dil_w_qkv, dil_w_o, w_ffn_up, w_ffn_down):
    b, t, d = x_prompt.shape
    db, nq, _ = x_sample.shape
    depth = g_mix_pre.shape[0]
    nh = d // HEAD_DIM
    hd = nh * HEAD_DIM
    ns = db * nq
    past_len = page_table.shape[1] * PAGE
    dil_states = (state_dil_kv_w128, state_dil_kv_w512, state_dil_kv_w2048)
    gheads = dil_states[0].shape[-2]
    gw = gheads * HEAD_DIM
    assert t % ATT_BLOCK == 0 and ns == PAGE
    rows_p = PROMPT_ROWS if t % PROMPT_ROWS == 0 else ATT_BLOCK
    ffn_rows = FFN_ROWS if (b * t) % FFN_ROWS == 0 else ATT_BLOCK

    sb_cacheT = _tokens_minor(cache_sb_kv).reshape(cache_sb_kv.shape[:2] + (2, hd, PAGE))
    fox_cacheT = _tokens_minor(cache_fox_kv).reshape(cache_fox_kv.shape[:2] + (2, hd, PAGE))
    fox_logfT = jnp.swapaxes(cache_fox_logf, 2, 3)
    dil_statesT = [_tokens_minor(s) for s in dil_states]

    yp = x_prompt
    ys = x_sample.reshape(1, ns, d)
    outs = {k: [] for k in ("sb_p", "sb_s", "fox_p", "fox_s", "fl_p", "fl_s")}
    dil_p = [[] for _ in DIL_PATTERNS]
    dil_s = [[] for _ in DIL_PATTERNS]
    slot = [0] * N_MIXERS
    for i in range(depth):
        kind = i % N_MIXERS
        li = slot[kind]
        slot[kind] += 1
        if kind == 2:
            w_qkv, w_o = dil_w_qkv[li], dil_w_o[li]
        elif kind == 1:
            w_qkv, w_o = fox_w_qkv[li], fox_w_o[li]
        else:
            w_qkv, w_o = sb_w_qkv[li], sb_w_o[li]
        nqf = w_qkv.shape[1] // 3
        w_o = w_o.astype(BF16)
        forget = (fox_w_f[li], fox_b_f[li]) if kind == 1 else None
        res_p = _qkv_proj(yp, g_mix_pre[i], w_qkv, ATT_BLOCK, "row" if kind == 0 else "col", forget)
        res_s = _qkv_proj(ys, g_mix_pre[i], w_qkv, ns, "row", forget)
        q_s = res_s[0].astype(F32).reshape(db, nq, nqf)
        kvT_s = res_s[1][0]
        if kind == 0:
            q_p, kvT_p, kvTb_p = res_p
            att_p = _sb_prompt_attention(q_p, kvTb_p)
            att_s = _sb_decode_attention(q_s, kvT_s, sb_cacheT, li, page_table)
            outs["sb_p"].append(_tokens_major(kvT_p, 2, nh, HEAD_DIM))
            outs["sb_s"].append(kvT_s.T.reshape(db, nq, 2, nh, HEAD_DIM))
        elif kind == 1:
            qT_p, k_p, kvT_p, vTb_p, lfT_p, cfb_p = res_p
            lfT_s = res_s[3][0]
            att_p = _fox_prompt_attention(qT_p, k_p, vTb_p, cfb_p)
            att_s = _fox_decode_attention(q_s, kvT_s, lfT_s, fox_cacheT, fox_logfT, li, page_table)
            outs["fox_p"].append(_tokens_major(kvT_p, 2, nh, HEAD_DIM))
            outs["fox_s"].append(kvT_s.T.reshape(db, nq, 2, nh, HEAD_DIM))
            outs["fl_p"].append(jnp.swapaxes(lfT_p, 1, 2))
            outs["fl_s"].append(lfT_s.T.reshape(db, nq, nh))
        else:
            assert t >= max(w for w, _ in DIL_PATTERNS)
            qT_p, k_p, kvT_p, vTb_p = res_p
            att_p = _dil_prompt_attention(qT_p, k_p, vTb_p)
            att_s, *new_states = _dil_decode_attention(
                q_s, kvT_s, [s[li] for s in dil_statesT], past_len)
            kv5 = kvT_p.reshape(b, 2, N_GROUPS, gw, t)
            for g, (win, _) in enumerate(DIL_PATTERNS):
                dil_p[g].append(_tokens_major(kv5[:, :, g, :, t - win:].reshape(b, 2 * gw, win),
                                              2, gheads, HEAD_DIM))
                dil_s[g].append(_tokens_major(new_states[g], 2, gheads, HEAD_DIM))
        att_s = att_s.reshape(1, ns, -1).astype(BF16)
        yp = _out_proj(att_p, w_o, g_mix_post[i], yp, rows_p, transposed=kind != 0)
        ys = _out_proj(att_s, w_o, g_mix_post[i], ys, ns)
        wup, wdn = w_ffn_up[i].astype(BF16), w_ffn_down[i].astype(BF16)
        yp = _ffn(yp.reshape(b * t, d), g_ffn_pre[i], wup, wdn, g_ffn_post[i],
                  ffn_rows, FFN_COLS).reshape(b, t, d)
        ys = _ffn(ys.reshape(ns, d), g_ffn_pre[i], wup, wdn, g_ffn_post[i],
                  ns, FFN_COLS).reshape(1, ns, d)
    return (yp, ys.reshape(db, nq, d),
            jnp.stack(outs["sb_p"]), jnp.stack(outs["sb_s"]),
            jnp.stack(outs["fox_p"]), jnp.stack(outs["fox_s"]),
            jnp.stack(outs["fl_p"]), jnp.stack(outs["fl_s"]),
            jnp.stack(dil_p[0]), jnp.stack(dil_s[0]),
            jnp.stack(dil_p[1]), jnp.stack(dil_s[1]),
            jnp.stack(dil_p[2]), jnp.stack(dil_s[2]))
```
